```python
import math
import jax, jax.numpy as jnp
from jax import lax
import numpy as np

D_MODEL = 2048
BATCH = 2
SEQ = 4096
DEPTH = 2
DEC_BATCH = 128
DEC_SEQ = 1
PAST_LEN = 2048
PAGE_SIZE = 128

HEAD_DIM = 128
H_GDN = 8
H_SB = 8
W_GDN = H_GDN * HEAD_DIM
W_SB = H_SB * HEAD_DIM
MIX_WIDTH = W_GDN + W_SB
CONV_K = 4
GDN_CHUNK = 64
SB_QBLOCK = 128
SB_BIAS_INIT = -6.0
D_FF = 5632
N_EXPERTS = 8
TOP_K = 2
D_FF_EXPERT = 5632
PLE_DIM = 256
N_DENSE = (DEPTH + 1) // 2
N_MOE = DEPTH // 2
DN_ALPHA = (2 * DEPTH) ** 0.25
DN_BETA = (8 * DEPTH) ** -0.25
LN_EPS = 1e-5
RMS_EPS = 1e-6
IN_DIM = 4 * W_GDN + 2 * H_GDN + 3 * W_SB
SPLITS = (3 * W_GDN, 4 * W_GDN, 4 * W_GDN + H_GDN, 4 * W_GDN + 2 * H_GDN,
          4 * W_GDN + 2 * H_GDN + W_SB, 4 * W_GDN + 2 * H_GDN + 2 * W_SB)

kernel_name = "hymba_gdn_stickbreak_deepnorm_step"

F32 = jnp.float32


def _layer_norm(x, g, b):
    xf = x.astype(F32)
    mu = jnp.mean(xf, -1, keepdims=True)
    var = jnp.mean(jnp.square(xf - mu), -1, keepdims=True)
    return ((xf - mu) * lax.rsqrt(var + LN_EPS) * g.astype(F32) + b.astype(F32)).astype(x.dtype)


def _rms_norm(x, g):
    return x * lax.rsqrt(jnp.mean(x * x, -1, keepdims=True) + RMS_EPS) * g.astype(F32)


def _l2norm(x):
    return x * lax.rsqrt(jnp.sum(x * x, -1, keepdims=True) + 1e-6)


def _gated_delta_rule(q, k, v, g, beta, s0):
    bsz, nh, L, dk = q.shape
    dv = v.shape[-1]
    C = min(GDN_CHUNK, L)
    pad = (-L) % C
    if pad:
        pw = ((0, 0), (0, 0), (0, pad), (0, 0))
        q, k, v = jnp.pad(q, pw), jnp.pad(k, pw), jnp.pad(v, pw)
        g = jnp.pad(g, ((0, 0), (0, 0), (0, pad)))
        beta = jnp.pad(beta, ((0, 0), (0, 0), (0, pad)))
    n = (L + pad) // C
    q, k, v = (t.reshape(bsz, nh, n, C, t.shape[-1]) for t in (q, k, v))
    g = jnp.cumsum(g.reshape(bsz, nh, n, C), axis=-1)
    beta = beta.reshape(bsz, nh, n, C)
    incl = jnp.tril(jnp.ones((C, C), bool))
    strict = jnp.tril(jnp.ones((C, C), bool), -1)
    diff = g[..., :, None] - g[..., None, :]
    decay = jnp.where(incl, jnp.exp(jnp.where(incl, diff, 0.0)), 0.0)
    kb = k * beta[..., None]
    vb = v * beta[..., None]
    lower = jnp.where(strict, jnp.einsum('bhncd,bhnsd->bhncs', kb, k) * decay, 0.0)
    m = lower + jnp.eye(C, dtype=lower.dtype)
    u = lax.linalg.triangular_solve(m, vb, left_side=True, lower=True, unit_diagonal=True)
    w = lax.linalg.triangular_solve(m, kb * jnp.exp(g)[..., None], left_side=True, lower=True,
                                    unit_diagonal=True)
    a_intra = jnp.einsum('bhncd,bhnsd->bhncs', q, k) * decay
    q_dec = q * jnp.exp(g)[..., None]
    g_last = g[..., -1]
    k_dec = k * jnp.exp(g_last[..., None] - g)[..., None]

    def step(s, inp):
        qd, kd, wn, un, an, gl = inp
        v_new = un - jnp.einsum('bhcd,bhde->bhce', wn, s)
        o = jnp.einsum('bhcd,bhde->bhce', qd, s) + jnp.einsum('bhcs,bhse->bhce', an, v_new)
        s = s * jnp.exp(gl)[..., None, None] + jnp.einsum('bhcd,bhce->bhde', kd, v_new)
        return s, o

    xs = tuple(jnp.moveaxis(t, 2, 0) for t in (q_dec, k_dec, w, u, a_intra, g_last))
    s_final, o = lax.scan(step, s0, xs)
    o = jnp.moveaxis(o, 0, 2).reshape(bsz, nh, n * C, dv)[:, :, :L]
    return o, s_final


def _stick_breaking(q, k, v, q_pos, k_pos, bias):
    z = jnp.einsum('bqhd,bkhd->bhqk', q, k) * (HEAD_DIM ** -0.5) + bias.astype(F32)[None, :, None, None]
    causal = k_pos[None, :] < q_pos[:, None]
    log_stay = jnp.where(causal, jax.nn.log_sigmoid(-z), 0.0)
    log_after = lax.cumsum(log_stay, axis=3, reverse=True) - log_stay
    a = jnp.where(causal, jnp.exp(jax.nn.log_sigmoid(z) + log_after), 0.0)
    return jnp.einsum('bhqk,bkhd->bqhd', a, v)


def _stick_breaking_blocked(q, k, v, q_pos, k_pos, bias):
    bsz, L, nh, d = q.shape
    if L <= SB_QBLOCK:
        return _stick_breaking(q, k, v, q_pos, k_pos, bias)
    nb = L // SB_QBLOCK
    qb = jnp.moveaxis(q.reshape(bsz, nb, SB_QBLOCK, nh, d), 1, 0)
    pb = q_pos.reshape(nb, SB_QBLOCK)
    ob = lax.map(lambda a: _stick_breaking(a[0], k, v, a[1], k_pos, bias), (qb, pb))
    return jnp.moveaxis(ob, 0, 1).reshape(bsz, L, nh, d)


def _token_mixer(x, w_in, conv_w, a_log, dt_bias, gdn_norm, sb_norm, sb_bias, w_out, conv_buf, s0,
                 k_past, v_past):
    bsz, L, _ = x.shape
    proj = jnp.einsum('bld,de->ble', x, w_in).astype(F32)
    qkv_a, z_a, b_a, a_a, q_b, k_b, v_b = jnp.split(proj, SPLITS, axis=-1)
    xp = jnp.concatenate([conv_buf.astype(F32), qkv_a], axis=1)
    cw = conv_w.astype(F32)
    conv = cw[0] * xp[:, 0:L]
    for i in range(1, CONV_K):
        conv = conv + cw[i] * xp[:, i:i + L]
    new_conv = xp[:, L:]
    qkv = jax.nn.silu(conv)
    q_a, k_a, v_a = jnp.split(qkv, 3, axis=-1)
    to_heads = lambda t: t.reshape(bsz, L, H_GDN, HEAD_DIM).transpose(0, 2, 1, 3)
    q_a = _l2norm(to_heads(q_a)) * (HEAD_DIM ** -0.5)
    k_a = _l2norm(to_heads(k_a))
    v_a = to_heads(v_a)
    beta = jax.nn.sigmoid(b_a).transpose(0, 2, 1)
    g = (-jnp.exp(a_log.astype(F32)) * jax.nn.softplus(a_a + dt_bias.astype(F32))).transpose(0, 2, 1)
    o_a, s_new = _gated_delta_rule(q_a, k_a, v_a, g, beta, s0.astype(F32))
    o_a = o_a.transpose(0, 2, 1, 3)
    o_a = _rms_norm(o_a, gdn_norm) * jax.nn.silu(z_a.reshape(bsz, L, H_GDN, HEAD_DIM))
    q_b = q_b.reshape(bsz, L, H_SB, HEAD_DIM)
    k_b = k_b.reshape(bsz, L, H_SB, HEAD_DIM)
    v_b = v_b.reshape(bsz, L, H_SB, HEAD_DIM)
    past = k_past.shape[1]
    k_all = jnp.concatenate([k_past.astype(F32), k_b], axis=1)
    v_all = jnp.concatenate([v_past.astype(F32), v_b], axis=1)
    q_pos = past + jnp.arange(L, dtype=jnp.int32)
    k_pos = jnp.arange(past + L, dtype=jnp.int32)
    o_b = _rms_norm(_stick_breaking_blocked(q_b, k_all, v_all, q_pos, k_pos, sb_bias), sb_norm)
    o = jnp.concatenate([o_a.reshape(bsz, L, W_GDN), o_b.reshape(bsz, L, W_SB)], axis=-1).astype(x.dtype)
    y = jnp.einsum('ble,ed->bld', o, w_out)
    return y, new_conv, s_new, k_b, v_b


def _swiglu(x, wg, wu, wd):
    return (jax.nn.silu(x @ wg) * (x @ wu)) @ wd


def _moe(x, router, wg, wu, wd):
    logits = jnp.einsum('bld,de->ble', x, router).astype(F32)
    top_v, top_i = lax.top_k(logits, TOP_K)
    top_w = jax.nn.softmax(top_v, axis=-1)
    gate = jnp.sum(jax.nn.one_hot(top_i, N_EXPERTS, dtype=F32) * top_w[..., None], axis=-2)
    y = jnp.zeros(x.shape, F32)
    for e in range(N_EXPERTS):
        y = y + gate[..., e:e + 1] * _swiglu(x, wg[e], wu[e], wd[e]).astype(F32)
    return y.astype(x.dtype)


def _trunk(x, p, conv0, s0, k_past, v_past, w_in, conv_w, a_log, dt_bias, gdn_norm, sb_norm, sb_bias, w_out,
           ln1_g, ln1_b, ln2_g, ln2_b, ffn_w_gate, ffn_w_up, ffn_w_down,
           moe_router, moe_w_gate, moe_w_up, moe_w_down, ple_w_gate, ple_w_proj):
    convs, states, ks, vs = [], [], [], []
    for l in range(DEPTH):
        h, c_new, s_new, k_new, v_new = _token_mixer(
            x, w_in[l], conv_w[l], a_log[l], dt_bias[l], gdn_norm[l], sb_norm[l], sb_bias[l], w_out[l],
            conv0[l], s0[l], k_past[l], v_past[l])
        x = _layer_norm(DN_ALPHA * x + h, ln1_g[l], ln1_b[l])
        if l % 2 == 0:
            f = _swiglu(x, ffn_w_gate[l // 2], ffn_w_up[l // 2], ffn_w_down[l // 2])
        else:
            f = _moe(x, moe_router[l // 2], moe_w_gate[l // 2], moe_w_up[l // 2], moe_w_down[l // 2])
        ple = jax.nn.sigmoid(x @ ple_w_gate[l]) * (p[l] @ ple_w_proj[l])
        x = _layer_norm(DN_ALPHA * x + f + ple, ln2_g[l], ln2_b[l])
        convs.append(c_new)
        states.append(s_new)
        ks.append(k_new)
        vs.append(v_new)
    return x, jnp.stack(convs), jnp.stack(states), jnp.stack(ks), jnp.stack(vs)


def setup_inputs(seed: int = 0) -> dict:
    key = jax.random.key(seed)
    ks = iter(jax.random.split(key, 48))
    nrm = lambda shape, scale: jax.random.normal(next(ks), shape, F32) * scale
    n_pages = PAST_LEN // PAGE_SIZE
    n_used = DEC_BATCH * n_pages
    n_pool = n_used + max(1, n_used // 4)
    page_table = jax.random.permutation(next(ks), n_pool)[:n_used].reshape(DEC_BATCH, n_pages).astype(jnp.int32)
    dt = jnp.exp(jax.random.uniform(next(ks), (DEPTH, H_GDN), F32, math.log(1e-3), math.log(1e-1)))
    dt_bias = dt + jnp.log(-jnp.expm1(-dt))
    a_log = jnp.log(jax.random.uniform(next(ks), (DEPTH, H_GDN), F32, 1.0, 16.0))
    return {
        "x_prompt": nrm((BATCH, SEQ, D_MODEL), 1.0),
        "x_sample": nrm((DEC_BATCH, DEC_SEQ, D_MODEL), 1.0),
        "cache_k": nrm((DEPTH, n_pool, PAGE_SIZE, H_SB, HEAD_DIM), 1.0),
        "cache_v": nrm((DEPTH, n_pool, PAGE_SIZE, H_SB, HEAD_DIM), 1.0),
        "state_conv": nrm((DEPTH, DEC_BATCH, CONV_K - 1, 3 * W_GDN), 1.0),
        "state_delta": nrm((DEPTH, DEC_BATCH, H_GDN, HEAD_DIM, HEAD_DIM), 0.5),
        "page_table": page_table,
        "p_prompt": nrm((DEPTH, BATCH, SEQ, PLE_DIM), 1.0),
        "p_sample": nrm((DEPTH, DEC_BATCH, DEC_SEQ, PLE_DIM), 1.0),
        "w_in": nrm((DEPTH, D_MODEL, IN_DIM), D_MODEL ** -0.5),
        "conv_w": nrm((DEPTH, CONV_K, 3 * W_GDN), CONV_K ** -0.5),
        "a_log": a_log,
        "dt_bias": dt_bias,
        "gdn_norm": 1.0 + nrm((DEPTH, HEAD_DIM), 0.02),
        "sb_norm": 1.0 + nrm((DEPTH, HEAD_DIM), 0.02),
        "sb_bias": SB_BIAS_INIT + nrm((DEPTH, H_SB), 0.5),
        "w_out": nrm((DEPTH, MIX_WIDTH, D_MODEL), DN_BETA * MIX_WIDTH ** -0.5),
        "ln1_g": 1.0 + nrm((DEPTH, D_MODEL), 0.02),
        "ln1_b": nrm((DEPTH, D_MODEL), 0.02),
        "ln2_g": 1.0 + nrm((DEPTH, D_MODEL), 0.02),
        "ln2_b": nrm((DEPTH, D_MODEL), 0.02),
        "ffn_w_gate": nrm((N_DENSE, D_MODEL, D_FF), D_MODEL ** -0.5),
        "ffn_w_up": nrm((N_DENSE, D_MODEL, D_FF), D_MODEL ** -0.5),
        "ffn_w_down": nrm((N_DENSE, D_FF, D_MODEL), DN_BETA * D_FF ** -0.5),
        "moe_router": nrm((N_MOE, D_MODEL, N_EXPERTS), D_MODEL ** -0.5),
        "moe_w_gate": nrm((N_MOE, N_EXPERTS, D_MODEL, D_FF_EXPERT), D_MODEL ** -0.5),
        "moe_w_up": nrm((N_MOE, N_EXPERTS, D_MODEL, D_FF_EXPERT), D_MODEL ** -0.5),
        "moe_w_down": nrm((N_MOE, N_EXPERTS, D_FF_EXPERT, D_MODEL), DN_BETA * D_FF_EXPERT ** -0.5),
        "ple_w_gate": nrm((DEPTH, D_MODEL, D_MODEL), D_MODEL ** -0.5),
        "ple_w_proj": nrm((DEPTH, PLE_DIM, D_MODEL), DN_BETA * PLE_DIM ** -0.5),
    }


def reference(x_prompt, x_sample, cache_k, cache_v, state_conv, state_delta, page_table, p_prompt, p_sample,
              w_in, conv_w, a_log, dt_bias, gdn_norm, sb_norm, sb_bias, w_out, ln1_g, ln1_b, ln2_g, ln2_b,
              ffn_w_gate, ffn_w_up, ffn_w_down, moe_router, moe_w_gate, moe_w_up, moe_w_down,
              ple_w_gate, ple_w_proj):
    weights = (w_in, conv_w, a_log, dt_bias, gdn_norm, sb_norm, sb_bias, w_out, ln1_g, ln1_b, ln2_g, ln2_b,
               ffn_w_gate, ffn_w_up, ffn_w_down, moe_router, moe_w_gate, moe_w_up, moe_w_down,
               ple_w_gate, ple_w_proj)
    bp = x_prompt.shape[0]
    conv0_p = jnp.zeros((DEPTH, bp, CONV_K - 1, 3 * W_GDN), F32)
    s0_p = jnp.zeros((DEPTH, bp, H_GDN, HEAD_DIM, HEAD_DIM), F32)
    kv0_p = jnp.zeros((DEPTH, bp, 0, H_SB, HEAD_DIM), F32)
    y_prompt, conv_p, delta_p, k_p, v_p = _trunk(x_prompt, p_prompt, conv0_p, s0_p, kv0_p, kv0_p, *weights)
    db, n_pages = page_table.shape
    k_past = cache_k[:, page_table].reshape(DEPTH, db, n_pages * PAGE_SIZE, H_SB, HEAD_DIM)
    v_past = cache_v[:, page_table].reshape(DEPTH, db, n_pages * PAGE_SIZE, H_SB, HEAD_DIM)
    y_sample, conv_s, delta_s, k_s, v_s = _trunk(x_sample, p_sample, state_conv, state_delta, k_past, v_past,
                                                  *weights)
    return (y_prompt, y_sample, conv_p, delta_p, k_p, v_p, conv_s, delta_s, k_s, v_s)
```

```python
import functools
import math

import jax
import jax.numpy as jnp
from jax import lax
from jax.experimental import pallas as pl
from jax.experimental.pallas import tpu as pltpu

F32 = jnp.float32
BF16 = jnp.bfloat16
HIGHEST = lax.Precision.HIGHEST

HEAD_DIM = 128
N_HEADS = 8
GROUP_W = N_HEADS * HEAD_DIM
CONV_K = 4
LN_EPS = 1e-5
RMS_EPS = 1e-6
L2_EPS = 1e-6
SUBLANES = 8
LANES = 128
VMEM_BYTES_V7X = 64 * 1024 * 1024
VMEM_LIMIT = VMEM_BYTES_V7X * 7 // 8

_NT = (((1,), (1,)), ((), ()))
_BNN = (((2,), (1,)), ((0,), (0,)))
_BNT = (((2,), (2,)), ((0,), (0,)))
_BTN = (((1,), (1,)), ((0,), (0,)))


def _pick_tile(n, target, mult):
    best = None
    for t in range(mult, min(n, target) + 1, mult):
        if n % t == 0:
            best = t
    assert best is not None, (n, target, mult)
    return best


def _params(sem):
    return pltpu.CompilerParams(dimension_semantics=sem, vmem_limit_bytes=VMEM_LIMIT)


def _silu(x):
    return x * jax.nn.sigmoid(x)


def _softplus(x):
    return jnp.maximum(x, 0.0) + jnp.log1p(jnp.exp(-jnp.abs(x)))


def _dot(a, b):
    return jnp.dot(a, b, preferred_element_type=F32)


def _layer_norm(r, g, b):
    mu = jnp.mean(r, axis=-1, keepdims=True)
    c = r - mu
    var = jnp.mean(c * c, axis=-1, keepdims=True)
    return c * lax.rsqrt(var + LN_EPS) * g + b


def _rms_norm(x, g):
    return x * lax.rsqrt(jnp.mean(x * x, axis=-1, keepdims=True) + RMS_EPS) * g


def _l2norm(x):
    return x * lax.rsqrt(jnp.sum(x * x, axis=-1, keepdims=True) + L2_EPS)


def _matmul_kernel(x_ref, w_ref, o_ref):
    o_ref[...] = _dot(x_ref[...], w_ref[0])


def _matmul(xb, w, layer, tn):
    t, k = xb.shape
    n = w.shape[2]
    tm = _pick_tile(t, 1040, 16)
    return pl.pallas_call(
        _matmul_kernel,
        grid=(t // tm, n // tn),
        in_specs=[
            pl.BlockSpec((tm, k), lambda i, j: (i, 0)),
            pl.BlockSpec((1, k, tn), lambda i, j: (layer, 0, j)),
        ],
        out_specs=pl.BlockSpec((tm, tn), lambda i, j: (i, j)),
        out_shape=jax.ShapeDtypeStruct((t, n), F32),
        compiler_params=_params(("parallel", "arbitrary")),
        name="in_proj",
    )(xb, w)


def _gdn_heads(qkv, gt, alog_ref, dtb_ref, layer):
    rows = qkv.shape[0]
    qs, ks, vs, betas, gls = [], [], [], [], []
    for h in range(N_HEADS):
        lo = h * HEAD_DIM
        qs.append(_l2norm(qkv[:, lo:lo + HEAD_DIM]) * (HEAD_DIM ** -0.5))
        ks.append(_l2norm(qkv[:, GROUP_W + lo:GROUP_W + lo + HEAD_DIM]))
        vs.append(qkv[:, 2 * GROUP_W + lo:2 * GROUP_W + lo + HEAD_DIM])
        b_logit = jnp.broadcast_to(gt[:, h:h + 1], (rows, HEAD_DIM))
        a_logit = jnp.broadcast_to(gt[:, N_HEADS + h:N_HEADS + h + 1], (rows, HEAD_DIM))
        betas.append(jax.nn.sigmoid(b_logit))
        rate = jnp.exp(jnp.full((1, HEAD_DIM), alog_ref[layer, h], F32))
        gls.append(-rate * _softplus(a_logit + dtb_ref[layer, h]))
    return tuple(jnp.stack(a, axis=0) for a in (qs, ks, vs, betas, gls))


def _gdn_prompt_kernel(alog_ref, dtb_ref, pa_ref, g_ref, cw_ref, norm_ref, o_ref, s_out_ref, s_ref, xp_ref, *,
                       layer, n_chunks):
    t = pl.program_id(1)
    c = pa_ref.shape[0]
    w3 = 3 * GROUP_W

    @pl.when(t == 0)
    def _():
        s_ref[...] = jnp.zeros_like(s_ref)
        xp_ref[0:SUBLANES, :] = jnp.zeros((SUBLANES, w3), F32)

    x = pa_ref[:, :w3]
    xp_ref[SUBLANES:SUBLANES + c, :] = x
    cw = cw_ref[0]
    acc = cw[CONV_K - 1:CONV_K, :] * x
    for back in range(1, CONV_K):
        acc = acc + cw[CONV_K - 1 - back:CONV_K - back, :] * xp_ref[SUBLANES - back:SUBLANES - back + c, :]
    xp_ref[0:SUBLANES, :] = x[c - SUBLANES:, :]
    qkv = _silu(acc)

    q, k, v, beta, gl = _gdn_heads(qkv, g_ref[...], alog_ref, dtb_ref, layer)

    row = lax.broadcasted_iota(jnp.int32, (c, c), 0)
    col = lax.broadcasted_iota(jnp.int32, (c, c), 1)
    incl = (col <= row)[None]
    strict = (col < row)[None]
    bshape = (N_HEADS, c, c)
    tri_incl = jnp.broadcast_to((col <= row).astype(F32)[None], bshape)
    pick0 = jnp.broadcast_to((col == 0).astype(F32)[None], bshape)
    eye = (row == col).astype(F32)[None]

    hdot = functools.partial(lax.dot_general, precision=HIGHEST, preferred_element_type=F32)
    gc = hdot(tri_incl, gl, _BNN)
    gc_row = hdot(pick0, gc, _BNT)
    decay = jnp.where(incl, jnp.exp(jnp.where(incl, gc - gc_row, 0.0)), 0.0)
    kb = k * beta
    vb = v * beta
    neg_a = -jnp.where(strict, hdot(kb, k, _BNT) * decay, 0.0)
    inv = eye + neg_a
    power = neg_a
    for _ in range(int(math.log2(c)) - 1):
        power = hdot(power, power, _BNN)
        inv = inv + hdot(inv, power, _BNN)
    eg = jnp.exp(gc)
    u = hdot(inv, vb, _BNN)
    w = hdot(inv, kb * eg, _BNN)
    a_intra = hdot(q, k, _BNT) * decay
    g_last = gc[:, c - 1:c, :]
    k_dec = k * jnp.exp(g_last - gc)

    s = s_ref[...]
    v_new = u - hdot(w, s, _BNN)
    o = hdot(q * eg, s, _BNN) + hdot(a_intra, v_new, _BNN)
    s_new = s * jnp.exp(g_last) + hdot(k_dec, v_new, _BTN)
    s_ref[...] = s_new

    nrm = norm_ref[0]
    for h in range(N_HEADS):
        lo = h * HEAD_DIM
        z = pa_ref[:, w3 + lo:w3 + lo + HEAD_DIM]
        o_ref[:, lo:lo + HEAD_DIM] = _rms_norm(o[h], nrm) * _silu(z)

    @pl.when(t == n_chunks - 1)
    def _():
        s_out_ref[0] = s_new


def _gdn_prompt(pa, pb, conv_w, a_log, dt_bias, gdn_norm3, layer, bsz, seq):
    c = HEAD_DIM
    nt = seq // c
    gate_blk = 3 * GROUP_W // LANES
    return pl.pallas_call(
        functools.partial(_gdn_prompt_kernel, layer=layer, n_chunks=nt),
        grid=(bsz, nt),
        in_specs=[
            pl.BlockSpec(memory_space=pltpu.SMEM),
            pl.BlockSpec(memory_space=pltpu.SMEM),
            pl.BlockSpec((c, 4 * GROUP_W), lambda b, t: (b * nt + t, 0)),
            pl.BlockSpec((c, LANES), lambda b, t: (b * nt + t, gate_blk)),
            pl.BlockSpec((1, CONV_K, 3 * GROUP_W), lambda b, t: (layer, 0, 0)),
            pl.BlockSpec((1, 1, HEAD_DIM), lambda b, t: (layer, 0, 0)),
        ],
        out_specs=[
            pl.BlockSpec((c, GROUP_W), lambda b, t: (b * nt + t, 0)),
            pl.BlockSpec((1, N_HEADS, HEAD_DIM, HEAD_DIM), lambda b, t: (b, 0, 0, 0)),
        ],
        out_shape=[
            jax.ShapeDtypeStruct((bsz * seq, GROUP_W), F32),
            jax.ShapeDtypeStruct((bsz, N_HEADS, HEAD_DIM, HEAD_DIM), F32),
        ],
        scratch_shapes=[
            pltpu.VMEM((N_HEADS, HEAD_DIM, HEAD_DIM), F32),
            pltpu.VMEM((SUBLANES + c, 3 * GROUP_W), F32),
        ],
        compiler_params=_params(("parallel", "arbitrary")),
        name="gdn_prompt",
    )(a_log, dt_bias, pa, pb, conv_w, gdn_norm3)


def _gdn_sample_kernel(alog_ref, dtb_ref, pa_ref, g_ref, sc_ref, cw_ref, norm_ref, s_ref, o_ref, nc_ref, s_out_ref, *,
                       layer):
    bb = pa_ref.shape[0]
    w3 = 3 * GROUP_W
    x = pa_ref[:, :w3]
    cw = cw_ref[0]
    acc = cw[CONV_K - 1:CONV_K, :] * x
    for i in range(CONV_K - 1):
        acc = acc + cw[i:i + 1, :] * sc_ref[0, i]
    for i in range(CONV_K - 2):
        nc_ref[i] = sc_ref[0, i + 1]
    nc_ref[CONV_K - 2] = x
    qkv = _silu(acc)

    q, k, v, beta, gl = _gdn_heads(qkv, g_ref[...], alog_ref, dtb_ref, layer)
    eg = jnp.exp(gl)
    kq_t = jnp.concatenate([k.reshape(N_HEADS * bb, HEAD_DIM), q.reshape(N_HEADS * bb, HEAD_DIM)], axis=0).T
    nrm = norm_ref[0]
    for h in range(N_HEADS):
        lo = h * HEAD_DIM
        z = pa_ref[:, w3 + lo:w3 + lo + HEAD_DIM]
        for b in range(bb):
            p = h * bb + b
            kcol = kq_t[:, p:p + 1]
            qcol = kq_t[:, N_HEADS * bb + p:N_HEADS * bb + p + 1]
            s = s_ref[0, b, h]
            ks = jnp.sum(kcol * s, axis=0, keepdims=True)
            qs = jnp.sum(qcol * s, axis=0, keepdims=True)
            e = eg[h, b:b + 1]
            bt = beta[h, b:b + 1]
            v_new = bt * v[h, b:b + 1] - bt * e * ks
            qk = jnp.sum(q[h, b:b + 1] * k[h, b:b + 1], axis=1, keepdims=True)
            o = e * qs + qk * v_new
            s_out_ref[0, b, h] = s * e + kcol * v_new
            o_ref[b:b + 1, lo:lo + HEAD_DIM] = _rms_norm(o, nrm) * _silu(z[b:b + 1])


def _gdn_sample(pa, pb, sc_t, state_delta, conv_w, a_log, dt_bias, gdn_norm3, layer, t_prompt, dbsz):
    bb = LANES // (2 * N_HEADS)
    assert dbsz % bb == 0 and t_prompt % bb == 0
    off = t_prompt // bb
    gate_blk = 3 * GROUP_W // LANES
    return pl.pallas_call(
        functools.partial(_gdn_sample_kernel, layer=layer),
        grid=(dbsz // bb,),
        in_specs=[
            pl.BlockSpec(memory_space=pltpu.SMEM),
            pl.BlockSpec(memory_space=pltpu.SMEM),
            pl.BlockSpec((bb, 4 * GROUP_W), lambda i: (off + i, 0)),
            pl.BlockSpec((bb, LANES), lambda i: (off + i, gate_blk)),
            pl.BlockSpec((1, CONV_K - 1, bb, 3 * GROUP_W), lambda i: (layer, 0, i, 0)),
            pl.BlockSpec((1, CONV_K, 3 * GROUP_W), lambda i: (layer, 0, 0)),
            pl.BlockSpec((1, 1, HEAD_DIM), lambda i: (layer, 0, 0)),
            pl.BlockSpec((1, bb, N_HEADS, HEAD_DIM, HEAD_DIM), lambda i: (layer, i, 0, 0, 0)),
        ],
        out_specs=[
            pl.BlockSpec((bb, GROUP_W), lambda i: (i, 0)),
            pl.BlockSpec((CONV_K - 1, bb, 3 * GROUP_W), lambda i: (0, i, 0)),
            pl.BlockSpec((1, bb, N_HEADS, HEAD_DIM, HEAD_DIM), lambda i: (0, i, 0, 0, 0)),
        ],
        out_shape=[
            jax.ShapeDtypeStruct((dbsz, GROUP_W), F32),
            jax.ShapeDtypeStruct((CONV_K - 1, dbsz, 3 * GROUP_W), F32),
            jax.ShapeDtypeStruct((1, dbsz, N_HEADS, HEAD_DIM, HEAD_DIM), F32),
        ],
        compiler_params=_params(("parallel",)),
        name="gdn_sample",
    )(a_log, dt_bias, pa, pb, sc_t, conv_w, gdn_norm3, state_delta)


def _sb_tile(z, causal, r, u_strict):
    log_stay = -_softplus(z)
    masked = log_stay if causal is None else jnp.where(causal, log_stay, 0.0)
    hi = masked.astype(BF16)
    lo = (masked - hi.astype(F32)).astype(BF16)
    within = _dot(hi, u_strict) + _dot(lo, u_strict)
    a = jnp.exp(z + log_stay + within + r)
    if causal is not None:
        a = jnp.where(causal, a, 0.0)
    return a, r + within[:, 0:1] + masked[:, 0:1]


def _sb_prompt_kernel(bias_ref, q_ref, k_ref, v_ref, norm_ref, o_ref, *, layer):
    h = pl.program_id(1)
    i = pl.program_id(2)
    tb = q_ref.shape[0]
    qb = q_ref[...].astype(BF16)
    bias = bias_ref[layer, h]
    row = lax.broadcasted_iota(jnp.int32, (tb, tb), 0)
    col = lax.broadcasted_iota(jnp.int32, (tb, tb), 1)
    u_strict = (row > col).astype(BF16)

    def body(jj, carry):
        r, acc = carry
        j = i - jj
        start = pl.multiple_of(j * tb, tb)
        kblk = k_ref[pl.ds(start, tb), :].astype(BF16)
        vblk = v_ref[pl.ds(start, tb), :].astype(BF16)
        z = lax.dot_general(qb, kblk, _NT, preferred_element_type=F32) * (HEAD_DIM ** -0.5) + bias
        causal = (col + j * tb) < (row + i * tb)
        a, r = _sb_tile(z, causal, r, u_strict)
        return r, acc + _dot(a.astype(BF16), vblk)

    init = (jnp.zeros((tb, 1), F32), jnp.zeros((tb, HEAD_DIM), F32))
    _, acc = lax.fori_loop(0, i + 1, body, init)
    o_ref[...] = _rms_norm(acc, norm_ref[0])


def _sb_prompt(pb, sb_bias, sb_norm3, layer, bsz, seq):
    tb = _pick_tile(seq, 128, LANES)
    nq = seq // tb
    return pl.pallas_call(
        functools.partial(_sb_prompt_kernel, layer=layer),
        grid=(bsz, N_HEADS, nq),
        in_specs=[
            pl.BlockSpec(memory_space=pltpu.SMEM),
            pl.BlockSpec((tb, HEAD_DIM), lambda b, h, i: (b * nq + i, h)),
            pl.BlockSpec((seq, HEAD_DIM), lambda b, h, i: (b, N_HEADS + h)),
            pl.BlockSpec((seq, HEAD_DIM), lambda b, h, i: (b, 2 * N_HEADS + h)),
            pl.BlockSpec((1, 1, HEAD_DIM), lambda b, h, i: (layer, 0, 0)),
        ],
        out_specs=pl.BlockSpec((tb, HEAD_DIM), lambda b, h, i: (b * nq + i, h)),
        out_shape=jax.ShapeDtypeStruct((bsz * seq, GROUP_W), F32),
        compiler_params=_params(("parallel", "parallel", "arbitrary")),
        name="sb_prompt",
    )(sb_bias, pb, pb, pb, sb_norm3)


def _sb_sample_kernel(pt_ref, q_ref, bias_ref, norm_ref, *refs, n_pages):
    del pt_ref
    k_refs = refs[:n_pages]
    v_refs = refs[n_pages:2 * n_pages]
    o_ref = refs[2 * n_pages]
    page = k_refs[0].shape[2]
    rows = 2 * SUBLANES
    q = jnp.broadcast_to(q_ref[0], (rows, GROUP_W))
    head_of_row = lax.broadcasted_iota(jnp.int32, (rows, GROUP_W), 0)
    head_of_lane = lax.broadcasted_iota(jnp.int32, (rows, GROUP_W), 1) // HEAD_DIM
    q_heads = jnp.where(head_of_row == head_of_lane, q, 0.0).astype(BF16)
    bias = bias_ref[0]
    prow = lax.broadcasted_iota(jnp.int32, (page, page), 0)
    pcol = lax.broadcasted_iota(jnp.int32, (page, page), 1)
    u_strict = (prow > pcol).astype(BF16)

    r = jnp.zeros((rows, 1), F32)
    acc = jnp.zeros((rows, GROUP_W), F32)
    for p in reversed(range(n_pages)):
        kp = k_refs[p][0, 0].astype(BF16)
        vp = v_refs[p][0, 0].astype(BF16)
        z = lax.dot_general(q_heads, kp, _NT, preferred_element_type=F32) * (HEAD_DIM ** -0.5) + bias
        a, r = _sb_tile(z, None, r, u_strict)
        acc = acc + _dot(a.astype(BF16), vp)
    nrm = norm_ref[0]
    for h in range(N_HEADS):
        lo = h * HEAD_DIM
        o_ref[0, :, lo:lo + HEAD_DIM] = _rms_norm(acc[h:h + 1, lo:lo + HEAD_DIM], nrm)


def _sb_sample(q3, cache_k4, cache_v4, page_table, bias_rows, sb_norm3, layer):
    dbsz, n_pages = page_table.shape
    page = cache_k4.shape[2]
    assert page == LANES

    def page_spec(p):
        return pl.BlockSpec((1, 1, page, GROUP_W), lambda b, pt: (layer, pt[b, p], 0, 0))

    grid_spec = pltpu.PrefetchScalarGridSpec(
        num_scalar_prefetch=1,
        grid=(dbsz,),
        in_specs=[
            pl.BlockSpec((1, 1, GROUP_W), lambda b, pt: (b, 0, 0)),
            pl.BlockSpec((1, 2 * SUBLANES, LANES), lambda b, pt: (layer, 0, 0)),
            pl.BlockSpec((1, 1, HEAD_DIM), lambda b, pt: (layer, 0, 0)),
        ] + [page_spec(p) for p in range(n_pages)] * 2,
        out_specs=pl.BlockSpec((1, 1, GROUP_W), lambda b, pt: (b, 0, 0)),
    )
    out = pl.pallas_call(
        functools.partial(_sb_sample_kernel, n_pages=n_pages),
        grid_spec=grid_spec,
        out_shape=jax.ShapeDtypeStruct((dbsz, 1, GROUP_W), F32),
        compiler_params=_params(("parallel",)),
        name="sb_sample",
    )(page_table, q3, bias_rows, sb_norm3, *([cache_k4] * n_pages), *([cache_v4] * n_pages))
    return out.reshape(dbsz, GROUP_W)


def _outproj_kernel(oa_ref, ob_ref, w_ref, x_ref, g_ref, b_ref, *rest, alpha):
    o_ref, ob16_ref = rest[-2:]
    y = _dot(oa_ref[...].astype(BF16), w_ref[0, :GROUP_W, :]) + _dot(ob_ref[...].astype(BF16), w_ref[0, GROUP_W:, :])
    xn = _layer_norm(alpha * x_ref[...] + y, g_ref[0], b_ref[0])
    o_ref[...] = xn
    ob16_ref[...] = xn.astype(BF16)


def _outproj(oa, ob, w_out_b, x, ln_g3, ln_b3, layer, alpha, row_off, into=None):
    t_all, d = x.shape
    n = oa.shape[0]
    tm = _pick_tile(n, 512, 16)
    assert row_off % tm == 0
    off = row_off // tm
    in_specs = [
        pl.BlockSpec((tm, GROUP_W), lambda i: (i, 0)),
        pl.BlockSpec((tm, GROUP_W), lambda i: (i, 0)),
        pl.BlockSpec((1, 2 * GROUP_W, d), lambda i: (layer, 0, 0)),
        pl.BlockSpec((tm, d), lambda i: (off + i, 0)),
        pl.BlockSpec((1, 1, d), lambda i: (layer, 0, 0)),
        pl.BlockSpec((1, 1, d), lambda i: (layer, 0, 0)),
    ]
    args = [oa, ob, w_out_b, x, ln_g3, ln_b3]
    aliases = {}
    if into is not None:
        in_specs += [pl.BlockSpec(memory_space=pl.ANY)] * 2
        aliases = {len(args): 0, len(args) + 1: 1}
        args += list(into)
    return pl.pallas_call(
        functools.partial(_outproj_kernel, alpha=alpha),
        grid=(n // tm,),
        in_specs=in_specs,
        out_specs=[pl.BlockSpec((tm, d), lambda i: (off + i, 0))] * 2,
        out_shape=[jax.ShapeDtypeStruct((t_all, d), F32), jax.ShapeDtypeStruct((t_all, d), BF16)],
        input_output_aliases=aliases,
        compiler_params=_params(("parallel",)),
        name="out_proj_ln",
    )(*args)


def _router_kernel(x_ref, w_ref, o_ref, *, n_experts):
    logits = lax.dot_general(x_ref[...], w_ref[0], (((1,), (0,)), ((), ())), precision=HIGHEST,
                             preferred_element_type=F32)
    lane = lax.broadcasted_iota(jnp.int32, logits.shape, 1).astype(F32)
    neg = jnp.float32(-jnp.inf)
    logits = jnp.where(lane < n_experts, logits, neg)
    m1 = jnp.max(logits, axis=1, keepdims=True)
    i1 = jnp.min(jnp.where(logits == m1, lane, float(LANES)), axis=1, keepdims=True)
    rest = jnp.where(lane == i1, neg, logits)
    m2 = jnp.max(rest, axis=1, keepdims=True)
    i2 = jnp.min(jnp.where(rest == m2, lane, float(LANES)), axis=1, keepdims=True)
    e2 = jnp.exp(m2 - m1)
    denom = 1.0 + e2
    o_ref[...] = jnp.where(lane == i1, 1.0 / denom, 0.0) + jnp.where(lane == i2, e2 / denom, 0.0)


def _router(x, router_pad, layer_idx, n_experts):
    t, d = x.shape
    tm = _pick_tile(t, 1040, 8)
    return pl.pallas_call(
        functools.partial(_router_kernel, n_experts=n_experts),
        grid=(t // tm,),
        in_specs=[
            pl.BlockSpec((tm, d), lambda i: (i, 0)),
            pl.BlockSpec((1, d, LANES), lambda i: (layer_idx, 0, 0)),
        ],
        out_specs=pl.BlockSpec((tm, LANES), lambda i: (i, 0)),
        out_shape=jax.ShapeDtypeStruct((t, LANES), F32),
        compiler_params=_params(("parallel",)),
        name="moe_router",
    )(x, router_pad)


def _ffn_kernel(xb_ref, gate_ref, wg_ref, wu_ref, wd_ref, o_ref):
    e = pl.program_id(1)
    j = pl.program_id(2)
    xb = xb_ref[...]
    g = _dot(xb, wg_ref[0].astype(BF16))
    u = _dot(xb, wu_ref[0].astype(BF16))
    gates = gate_ref[...]
    lane = lax.broadcasted_iota(jnp.int32, gates.shape, 1)
    gate_e = jnp.sum(jnp.where(lane == e, gates, 0.0), axis=1, keepdims=True)
    hidden = (_silu(g) * u * gate_e).astype(BF16)
    contrib = _dot(hidden, wd_ref[0].astype(BF16))
    first = jnp.logical_and(e == 0, j == 0)

    @pl.when(first)
    def _():
        o_ref[...] = contrib

    @pl.when(jnp.logical_not(first))
    def _():
        o_ref[...] += contrib


def _ffn(xb, gates, wg, wu, wd):
    t, d = xb.shape
    n_e, _, f = wg.shape
    tm = _pick_tile(t, 832, 16)
    tf = _pick_tile(f, 512, LANES)
    return pl.pallas_call(
        _ffn_kernel,
        grid=(t // tm, n_e, f // tf),
        in_specs=[
            pl.BlockSpec((tm, d), lambda i, e, j: (i, 0)),
            pl.BlockSpec((tm, LANES), lambda i, e, j: (i, 0)),
            pl.BlockSpec((1, d, tf), lambda i, e, j: (e, 0, j)),
            pl.BlockSpec((1, d, tf), lambda i, e, j: (e, 0, j)),
            pl.BlockSpec((1, tf, d), lambda i, e, j: (e, j, 0)),
        ],
        out_specs=pl.BlockSpec((tm, d), lambda i, e, j: (i, 0)),
        out_shape=jax.ShapeDtypeStruct((t, d), F32),
        compiler_params=_params(("parallel", "arbitrary", "arbitrary")),
        name="ffn",
    )(xb, gates, wg, wu, wd)


def _final_kernel(x_ref, f_ref, p_ref, wg_ref, wp_ref, g_ref, b_ref, o_ref, ob16_ref, *, alpha):
    x = x_ref[...]
    gate = jax.nn.sigmoid(_dot(x.astype(BF16), wg_ref[0]))
    proj = _dot(p_ref[0].astype(BF16), wp_ref[0].astype(BF16))
    xn = _layer_norm(alpha * x + f_ref[...] + gate * proj, g_ref[0], b_ref[0])
    o_ref[...] = xn
    ob16_ref[...] = xn.astype(BF16)


def _final(x, f, p_all, ple_wg_b, ple_wp, ln_g3, ln_b3, layer, alpha):
    t, d = x.shape
    pd = p_all.shape[2]
    tm = _pick_tile(t, 320, 16)
    return pl.pallas_call(
        functools.partial(_final_kernel, alpha=alpha),
        grid=(t // tm,),
        in_specs=[
            pl.BlockSpec((tm, d), lambda i: (i, 0)),
            pl.BlockSpec((tm, d), lambda i: (i, 0)),
            pl.BlockSpec((1, tm, pd), lambda i: (layer, i, 0)),
            pl.BlockSpec((1, d, d), lambda i: (layer, 0, 0)),
            pl.BlockSpec((1, pd, d), lambda i: (layer, 0, 0)),
            pl.BlockSpec((1, 1, d), lambda i: (layer, 0, 0)),
            pl.BlockSpec((1, 1, d), lambda i: (layer, 0, 0)),
        ],
        out_specs=[pl.BlockSpec((tm, d), lambda i: (i, 0))] * 2,
        out_shape=[jax.ShapeDtypeStruct((t, d), F32), jax.ShapeDtypeStruct((t, d), BF16)],
        compiler_params=_params(("parallel",)),
        name="ple_ln",
    )(x, f, p_all, ple_wg_b, ple_wp, ln_g3, ln_b3)


def kernel(x_prompt, x_sample, cache_k, cache_v, state_conv, state_delta, page_table, p_prompt, p_sample, w_in, conv_w, a_log, dt_bias, gdn_norm, sb_norm, sb_bias, w_out, ln1_g, ln1_b, ln2_g, ln2_b, ffn_w_gate, ffn_w_up, ffn_w_down, moe_router, moe_w_gate, moe_w_up, moe_w_down, ple_w_gate, ple_w_proj):
    depth, d, in_dim = w_in.shape
    bsz, seq, _ = x_prompt.shape
    dbsz = x_sample.shape[0]
    assert x_sample.shape[1] == 1 and seq % HEAD_DIM == 0
    assert in_dim == 7 * GROUP_W + 2 * N_HEADS and w_out.shape[1] == 2 * GROUP_W
    tp = bsz * seq
    n_experts = moe_router.shape[2]
    alpha = (2 * depth) ** 0.25

    x = jnp.concatenate([x_prompt.reshape(tp, d), x_sample.reshape(dbsz, d)], axis=0)
    xb = x.astype(BF16)
    p_all = jnp.concatenate([p_prompt.reshape(depth, tp, -1), p_sample.reshape(depth, dbsz, -1)], axis=1)

    gate_lo = 4 * GROUP_W
    sb_lo = gate_lo + 2 * N_HEADS
    nb = 3 * GROUP_W + 2 * 256
    w_a = w_in[:, :, :gate_lo].astype(BF16)
    w_b = jnp.concatenate(
        [w_in[:, :, sb_lo:], w_in[:, :, gate_lo:sb_lo], jnp.zeros((depth, d, nb - 3 * GROUP_W - 2 * N_HEADS), F32)],
        axis=-1).astype(BF16)
    w_out_b = w_out.astype(BF16)
    ple_wg_b = ple_w_gate.astype(BF16)
    router_pad = jnp.pad(moe_router, ((0, 0), (0, 0), (0, LANES - n_experts)))
    bias_rows = jnp.broadcast_to(
        jnp.pad(sb_bias, ((0, 0), (0, 2 * SUBLANES - N_HEADS)))[:, :, None], (depth, 2 * SUBLANES, LANES))
    sc_t = jnp.swapaxes(state_conv, 1, 2)
    cache_k4 = cache_k.reshape(cache_k.shape[:3] + (GROUP_W,))
    cache_v4 = cache_v.reshape(cache_v.shape[:3] + (GROUP_W,))
    r3 = lambda a: a.reshape(depth, 1, a.shape[-1])
    gdn_norm3, sb_norm3 = r3(gdn_norm), r3(sb_norm)
    ln1_g3, ln1_b3, ln2_g3, ln2_b3 = r3(ln1_g), r3(ln1_b), r3(ln2_g), r3(ln2_b)
    ones_gate = jnp.ones((tp + dbsz, LANES), F32)

    conv_p, delta_p, k_p, v_p, conv_s, delta_s, k_s, v_s = ([] for _ in range(8))
    for l in range(depth):
        pa = _matmul(xb, w_a, l, tn=2048)
        pb = _matmul(xb, w_b, l, tn=nb // 2)

        oa_p, s_p = _gdn_prompt(pa, pb, conv_w, a_log, dt_bias, gdn_norm3, l, bsz, seq)
        ob_p = _sb_prompt(pb, sb_bias, sb_norm3, l, bsz, seq)
        oa_s, nc_s, s_s = _gdn_sample(pa, pb, sc_t, state_delta, conv_w, a_log, dt_bias, gdn_norm3, l, tp, dbsz)
        q_s = pb[tp:, :GROUP_W].reshape(dbsz, 1, GROUP_W)
        ob_s = _sb_sample(q_s, cache_k4, cache_v4, page_table, bias_rows, sb_norm3, l)

        bufs = _outproj(oa_p, ob_p, w_out_b, x, ln1_g3, ln1_b3, l, alpha, 0)
        x1, x1b = _outproj(oa_s, ob_s, w_out_b, x, ln1_g3, ln1_b3, l, alpha, tp, into=bufs)

        if l % 2 == 0:
            i = l // 2
            f = _ffn(x1b, ones_gate, ffn_w_gate[i:i + 1], ffn_w_up[i:i + 1], ffn_w_down[i:i + 1])
        else:
            i = l // 2
            gates = _router(x1, router_pad, i, n_experts)
            f = _ffn(x1b, gates, moe_w_gate[i], moe_w_up[i], moe_w_down[i])
        x, xb = _final(x1, f, p_all, ple_wg_b, ple_w_proj, ln2_g3, ln2_b3, l, alpha)

        pa_p = pa[:tp, :3 * GROUP_W].reshape(bsz, seq, 3 * GROUP_W)
        conv_p.append(pa_p[:, seq - (CONV_K - 1):])
        delta_p.append(s_p)
        k_p.append(pb[:tp, GROUP_W:2 * GROUP_W].reshape(bsz, seq, N_HEADS, HEAD_DIM))
        v_p.append(pb[:tp, 2 * GROUP_W:3 * GROUP_W].reshape(bsz, seq, N_HEADS, HEAD_DIM))
        conv_s.append(jnp.swapaxes(nc_s, 0, 1))
        delta_s.append(s_s[0])
        k_s.append(pb[tp:, GROUP_W:2 * GROUP_W].reshape(dbsz, 1, N_HEADS, HEAD_DIM))
        v_s.append(pb[tp:, 2 * GROUP_W:3 * GROUP_W].reshape(dbsz, 1, N_HEADS, HEAD_DIM))

    y_prompt = x[:tp].reshape(bsz, seq, d)
    y_sample = x[tp:].reshape(dbsz, 1, d)
    st = jnp.stack
    return (y_prompt, y_sample, st(conv_p), st(delta_p), st(k_p), st(v_p), st(conv_s), st(delta_s), st(k_s), st(v_s))
```

```python
import functools
import math

import jax
import jax.numpy as jnp
from jax import lax
from jax.experimental import pallas as pl
from jax.experimental.pallas import tpu as pltpu

F32 = jnp.float32
BF16 = jnp.bfloat16
HIGHEST = lax.Precision.HIGHEST

HEAD_DIM = 128
N_HEADS = 8
GROUP_W = N_HEADS * HEAD_DIM
CONV_K = 4
LN_EPS = 1e-5
RMS_EPS = 1e-6
L2_EPS = 1e-6
SUBLANES = 8
LANES = 128
VMEM_BYTES_V7X = 64 * 1024 * 1024
VMEM_LIMIT = VMEM_BYTES_V7X * 7 // 8

_NT = (((1,), (1,)), ((), ()))
_BNN = (((2,), (1,)), ((0,), (0,)))
_BNT = (((2,), (2,)), ((0,), (0,)))
_BTN = (((1,), (1,)), ((0,), (0,)))


def _pick_tile(n, target, mult):
    best = None
    for t in range(mult, min(n, target) + 1, mult):
        if n % t == 0:
            best = t
    assert best is not None, (n, target, mult)
    return best


def _params(sem):
    return pltpu.CompilerParams(dimension_semantics=sem, vmem_limit_bytes=VMEM_LIMIT)


def _silu(x):
    return x * jax.nn.sigmoid(x)


def _softplus(x):
    return jnp.maximum(x, 0.0) + jnp.log1p(jnp.exp(-jnp.abs(x)))


def _dot(a, b):
    return jnp.dot(a, b, preferred_element_type=F32)


def _split_bf16(a):
    hi = a.astype(BF16)
    return hi, (a - hi.astype(F32)).astype(BF16)


def _dot1(a, b, dims):
    return lax.dot_general(a.astype(BF16), b.astype(BF16), dims, preferred_element_type=F32)


def _dot3(a_parts, b_parts, dims):
    (a_hi, a_lo), (b_hi, b_lo) = a_parts, b_parts
    dot = functools.partial(lax.dot_general, dimension_numbers=dims, preferred_element_type=F32)
    return dot(a_hi, b_hi) + dot(a_hi, b_lo) + dot(a_lo, b_hi)


def _layer_norm(r, g, b):
    mu = jnp.mean(r, axis=-1, keepdims=True)
    c = r - mu
    var = jnp.mean(c * c, axis=-1, keepdims=True)
    return c * lax.rsqrt(var + LN_EPS) * g + b


def _rms_norm(x, g):
    return x * lax.rsqrt(jnp.mean(x * x, axis=-1, keepdims=True) + RMS_EPS) * g


def _l2norm(x):
    return x * lax.rsqrt(jnp.sum(x * x, axis=-1, keepdims=True) + L2_EPS)


def _matmul_kernel(x_ref, w_ref, o_ref):
    o_ref[...] = _dot(x_ref[...], w_ref[0])


def _matmul(xb, w, layer, tn):
    t, k = xb.shape
    n = w.shape[2]
    tm = _pick_tile(t, 1040, 16)
    return pl.pallas_call(
        _matmul_kernel,
        grid=(t // tm, n // tn),
        in_specs=[
            pl.BlockSpec((tm, k), lambda i, j: (i, 0)),
            pl.BlockSpec((1, k, tn), lambda i, j: (layer, 0, j)),
        ],
        out_specs=pl.BlockSpec((tm, tn), lambda i, j: (i, j)),
        out_shape=jax.ShapeDtypeStruct((t, n), F32),
        compiler_params=_params(("parallel", "arbitrary")),
        name="in_proj",
    )(xb, w)


def _gdn_heads(qkv, gt, alog_ref, dtb_ref, layer):
    rows = qkv.shape[0]
    qs, ks, vs, betas, gls = [], [], [], [], []
    for h in range(N_HEADS):
        lo = h * HEAD_DIM
        qs.append(_l2norm(qkv[:, lo:lo + HEAD_DIM]) * (HEAD_DIM ** -0.5))
        ks.append(_l2norm(qkv[:, GROUP_W + lo:GROUP_W + lo + HEAD_DIM]))
        vs.append(qkv[:, 2 * GROUP_W + lo:2 * GROUP_W + lo + HEAD_DIM])
        b_logit = jnp.broadcast_to(gt[:, h:h + 1], (rows, HEAD_DIM))
        a_logit = jnp.broadcast_to(gt[:, N_HEADS + h:N_HEADS + h + 1], (rows, HEAD_DIM))
        betas.append(jax.nn.sigmoid(b_logit))
        rate = jnp.exp(jnp.full((1, HEAD_DIM), alog_ref[layer, h], F32))
        gls.append(-rate * _softplus(a_logit + dtb_ref[layer, h]))
    return tuple(jnp.stack(a, axis=0) for a in (qs, ks, vs, betas, gls))


def _gdn_prompt_kernel(alog_ref, dtb_ref, pa_ref, g_ref, cw_ref, norm_ref, o_ref, s_out_ref, s_ref, xp_ref, *,
                       layer, n_chunks):
    t = pl.program_id(1)
    c = pa_ref.shape[0]
    w3 = 3 * GROUP_W

    @pl.when(t == 0)
    def _():
        s_ref[...] = jnp.zeros_like(s_ref)
        xp_ref[0:SUBLANES, :] = jnp.zeros((SUBLANES, w3), F32)

    x = pa_ref[:, :w3]
    xp_ref[SUBLANES:SUBLANES + c, :] = x
    cw = cw_ref[0]
    acc = cw[CONV_K - 1:CONV_K, :] * x
    for back in range(1, CONV_K):
        acc = acc + cw[CONV_K - 1 - back:CONV_K - back, :] * xp_ref[SUBLANES - back:SUBLANES - back + c, :]
    xp_ref[0:SUBLANES, :] = x[c - SUBLANES:, :]
    qkv = _silu(acc)

    q, k, v, beta, gl = _gdn_heads(qkv, g_ref[...], alog_ref, dtb_ref, layer)

    row = lax.broadcasted_iota(jnp.int32, (c, c), 0)
    col = lax.broadcasted_iota(jnp.int32, (c, c), 1)
    incl = (col <= row)[None]
    strict = (col < row)[None]
    bshape = (N_HEADS, c, c)
    tri_incl = jnp.broadcast_to((col <= row).astype(F32)[None], bshape)
    pick0 = jnp.broadcast_to((col == 0).astype(F32)[None], bshape)
    eye = (row == col).astype(F32)[None]

    hdot = functools.partial(lax.dot_general, precision=HIGHEST, preferred_element_type=F32)
    gc = hdot(tri_incl, gl, _BNN)
    gc_row = hdot(pick0, gc, _BNT)
    decay = jnp.where(incl, jnp.exp(jnp.where(incl, gc - gc_row, 0.0)), 0.0)
    kb = k * beta
    vb = v * beta
    neg_a = -jnp.where(strict, _dot1(kb, k, _BNT) * decay, 0.0)
    inv = eye + neg_a
    power = _split_bf16(neg_a)
    for _ in range(int(math.log2(c)) - 1):
        power = _split_bf16(_dot3(power, power, _BNN))
        inv = inv + _dot3(_split_bf16(inv), power, _BNN)
    inv = _split_bf16(inv)
    eg = jnp.exp(gc)
    u = _dot3(inv, _split_bf16(vb), _BNN)
    w = _dot3(inv, _split_bf16(kb * eg), _BNN)
    a_intra = _dot1(q, k, _BNT) * decay
    g_last = gc[:, c - 1:c, :]
    k_dec = k * jnp.exp(g_last - gc)

    s = s_ref[...]
    v_new = u - _dot1(w, s, _BNN)
    o = _dot1(q * eg, s, _BNN) + _dot1(a_intra, v_new, _BNN)
    s_new = s * jnp.exp(g_last) + _dot1(k_dec, v_new, _BTN)
    s_ref[...] = s_new

    nrm = norm_ref[0]
    for h in range(N_HEADS):
        lo = h * HEAD_DIM
        z = pa_ref[:, w3 + lo:w3 + lo + HEAD_DIM]
        o_ref[:, lo:lo + HEAD_DIM] = _rms_norm(o[h], nrm) * _silu(z)

    @pl.when(t == n_chunks - 1)
    def _():
        s_out_ref[0] = s_new


def _gdn_prompt(pa, pb, conv_w, a_log, dt_bias, gdn_norm3, layer, bsz, seq):
    c = HEAD_DIM
    nt = seq // c
    gate_blk = 3 * GROUP_W // LANES
    return pl.pallas_call(
        functools.partial(_gdn_prompt_kernel, layer=layer, n_chunks=nt),
        grid=(bsz, nt),
        in_specs=[
            pl.BlockSpec(memory_space=pltpu.SMEM),
            pl.BlockSpec(memory_space=pltpu.SMEM),
            pl.BlockSpec((c, 4 * GROUP_W), lambda b, t: (b * nt + t, 0)),
            pl.BlockSpec((c, LANES), lambda b, t: (b * nt + t, gate_blk)),
            pl.BlockSpec((1, CONV_K, 3 * GROUP_W), lambda b, t: (layer, 0, 0)),
            pl.BlockSpec((1, 1, HEAD_DIM), lambda b, t: (layer, 0, 0)),
        ],
        out_specs=[
            pl.BlockSpec((c, GROUP_W), lambda b, t: (b * nt + t, 0)),
            pl.BlockSpec((1, N_HEADS, HEAD_DIM, HEAD_DIM), lambda b, t: (b, 0, 0, 0)),
        ],
        out_shape=[
            jax.ShapeDtypeStruct((bsz * seq, GROUP_W), F32),
            jax.ShapeDtypeStruct((bsz, N_HEADS, HEAD_DIM, HEAD_DIM), F32),
        ],
        scratch_shapes=[
            pltpu.VMEM((N_HEADS, HEAD_DIM, HEAD_DIM), F32),
            pltpu.VMEM((SUBLANES + c, 3 * GROUP_W), F32),
        ],
        compiler_params=_params(("parallel", "arbitrary")),
        name="gdn_prompt",
    )(a_log, dt_bias, pa, pb, conv_w, gdn_norm3)


def _gdn_sample_kernel(alog_ref, dtb_ref, pa_ref, g_ref, sc_ref, cw_ref, norm_ref, s_ref, o_ref, nc_ref, s_out_ref, *,
                       layer):
    bb = pa_ref.shape[0]
    w3 = 3 * GROUP_W
    x = pa_ref[:, :w3]
    cw = cw_ref[0]
    acc = cw[CONV_K - 1:CONV_K, :] * x
    for i in range(CONV_K - 1):
        acc = acc + cw[i:i + 1, :] * sc_ref[0, i]
    for i in range(CONV_K - 2):
        nc_ref[i] = sc_ref[0, i + 1]
    nc_ref[CONV_K - 2] = x
    qkv = _silu(acc)

    q, k, v, beta, gl = _gdn_heads(qkv, g_ref[...], alog_ref, dtb_ref, layer)
    eg = jnp.exp(gl)
    kq_t = jnp.concatenate([k.reshape(N_HEADS * bb, HEAD_DIM), q.reshape(N_HEADS * bb, HEAD_DIM)], axis=0).T
    nrm = norm_ref[0]
    for h in range(N_HEADS):
        lo = h * HEAD_DIM
        z = pa_ref[:, w3 + lo:w3 + lo + HEAD_DIM]
        for b in range(bb):
            p = h * bb + b
            kcol = kq_t[:, p:p + 1]
            qcol = kq_t[:, N_HEADS * bb + p:N_HEADS * bb + p + 1]
            s = s_ref[0, b, h]
            ks = jnp.sum(kcol * s, axis=0, keepdims=True)
            qs = jnp.sum(qcol * s, axis=0, keepdims=True)
            e = eg[h, b:b + 1]
            bt = beta[h, b:b + 1]
            v_new = bt * v[h, b:b + 1] - bt * e * ks
            qk = jnp.sum(q[h, b:b + 1] * k[h, b:b + 1], axis=1, keepdims=True)
            o = e * qs + qk * v_new
            s_out_ref[0, b, h] = s * e + kcol * v_new
            o_ref[b:b + 1, lo:lo + HEAD_DIM] = _rms_norm(o, nrm) * _silu(z[b:b + 1])


def _gdn_sample(pa, pb, sc_t, state_delta, conv_w, a_log, dt_bias, gdn_norm3, layer, t_prompt, dbsz):
    bb = LANES // (2 * N_HEADS)
    assert dbsz % bb == 0 and t_prompt % bb == 0
    off = t_prompt // bb
    gate_blk = 3 * GROUP_W // LANES
    return pl.pallas_call(
        functools.partial(_gdn_sample_kernel, layer=layer),
        grid=(dbsz // bb,),
        in_specs=[
            pl.BlockSpec(memory_space=pltpu.SMEM),
            pl.BlockSpec(memory_space=pltpu.SMEM),
            pl.BlockSpec((bb, 4 * GROUP_W), lambda i: (off + i, 0)),
            pl.BlockSpec((bb, LANES), lambda i: (off + i, gate_blk)),
            pl.BlockSpec((1, CONV_K - 1, bb, 3 * GROUP_W), lambda i: (layer, 0, i, 0)),
            pl.BlockSpec((1, CONV_K, 3 * GROUP_W), lambda i: (layer, 0, 0)),
            pl.BlockSpec((1, 1, HEAD_DIM), lambda i: (layer, 0, 0)),
            pl.BlockSpec((1, bb, N_HEADS, HEAD_DIM, HEAD_DIM), lambda i: (layer, i, 0, 0, 0)),
        ],
        out_specs=[
            pl.BlockSpec((bb, GROUP_W), lambda i: (i, 0)),
            pl.BlockSpec((CONV_K - 1, bb, 3 * GROUP_W), lambda i: (0, i, 0)),
            pl.BlockSpec((1, bb, N_HEADS, HEAD_DIM, HEAD_DIM), lambda i: (0, i, 0, 0, 0)),
        ],
        out_shape=[
            jax.ShapeDtypeStruct((dbsz, GROUP_W), F32),
            jax.ShapeDtypeStruct((CONV_K - 1, dbsz, 3 * GROUP_W), F32),
            jax.ShapeDtypeStruct((1, dbsz, N_HEADS, HEAD_DIM, HEAD_DIM), F32),
        ],
        compiler_params=_params(("parallel",)),
        name="gdn_sample",
    )(a_log, dt_bias, pa, pb, sc_t, conv_w, gdn_norm3, state_delta)


def _suffix_sum_matrix(keys):
    row = lax.broadcasted_iota(jnp.int32, (keys, 2 * keys), 0)
    col = lax.broadcasted_iota(jnp.int32, (keys, 2 * keys), 1)
    return jnp.logical_or(row > col, col >= keys).astype(BF16)


def _stick_spend(z):
    return jnp.maximum(z, 0.0) + jnp.log(1.0 + jnp.exp(-jnp.abs(z)))


def _sb_weights(z, spend, spend_masked, r, sum_mat, n_tiles, axis):
    keys = sum_mat.shape[0]
    size = z.shape[axis] // n_tiles

    def tile(m, c):
        return lax.slice_in_dim(m, c * size, (c + 1) * size, axis=axis)

    rows = tile(z, 0).shape[0]
    hi, lo = _split_bf16(spend_masked)
    stack = lambda m: jnp.concatenate([tile(m, c) for c in range(n_tiles)], axis=0)
    sums = _dot(jnp.concatenate([stack(hi), stack(lo)], axis=1), jnp.concatenate([sum_mat, sum_mat], axis=0))
    parts = [None] * n_tiles
    for c in reversed(range(n_tiles)):
        s_c = sums[c * rows:(c + 1) * rows]
        parts[c] = jnp.exp(tile(z, c) - tile(spend, c) - s_c[:, :keys] - r)
        r = r + s_c[:, keys:]
    return parts, r


def _sb_prompt_kernel(bias_ref, q_ref, k_ref, v_ref, norm_ref, o_ref, *, layer, chunk):
    h = pl.program_id(1)
    i = pl.program_id(2)
    tq = q_ref.shape[0]
    n_sub = tq // chunk
    qb = (q_ref[...] * (HEAD_DIM ** -0.5)).astype(BF16)
    bias = bias_ref[layer, h]
    sum_mat = _suffix_sum_matrix(chunk)

    def block(start, r, acc, causal):
        kblk = k_ref[pl.ds(start, tq), :].astype(BF16)
        vblk = v_ref[pl.ds(start, tq), :].astype(BF16)
        z = lax.dot_general(qb, kblk, _NT, preferred_element_type=F32) + bias
        spend = _stick_spend(z)
        masked = spend if causal is None else jnp.where(causal, spend, 0.0)
        parts, r = _sb_weights(z, spend, masked, r, sum_mat, n_sub, 1)
        a = jnp.concatenate(parts, axis=1)
        if causal is not None:
            a = jnp.where(causal, a, 0.0)
        return r, acc + _dot(a.astype(BF16), vblk)

    row = lax.broadcasted_iota(jnp.int32, (tq, tq), 0)
    col = lax.broadcasted_iota(jnp.int32, (tq, tq), 1)
    init = (jnp.zeros((tq, chunk), F32), jnp.zeros((tq, HEAD_DIM), F32))
    carry = block(pl.multiple_of(i * tq, tq), *init, col < row)

    def body(jj, carry):
        return block(pl.multiple_of((i - 1 - jj) * tq, tq), *carry, None)

    _, acc = lax.fori_loop(0, i, body, carry)
    o_ref[...] = _rms_norm(acc, norm_ref[0])


def _sb_prompt(pb, sb_bias, sb_norm3, layer, bsz, seq):
    tq = _pick_tile(seq, 512, LANES)
    nq = seq // tq
    return pl.pallas_call(
        functools.partial(_sb_prompt_kernel, layer=layer, chunk=LANES),
        grid=(bsz, N_HEADS, nq),
        in_specs=[
            pl.BlockSpec(memory_space=pltpu.SMEM),
            pl.BlockSpec((tq, HEAD_DIM), lambda b, h, i: (b * nq + i, h)),
            pl.BlockSpec((seq, HEAD_DIM), lambda b, h, i: (b, N_HEADS + h)),
            pl.BlockSpec((seq, HEAD_DIM), lambda b, h, i: (b, 2 * N_HEADS + h)),
            pl.BlockSpec((1, 1, HEAD_DIM), lambda b, h, i: (layer, 0, 0)),
        ],
        out_specs=pl.BlockSpec((tq, HEAD_DIM), lambda b, h, i: (b * nq + i, h)),
        out_shape=jax.ShapeDtypeStruct((bsz * seq, GROUP_W), F32),
        compiler_params=_params(("parallel", "parallel", "arbitrary")),
        name="sb_prompt",
    )(sb_bias, pb, pb, pb, sb_norm3)


def _sb_sample_kernel(pt_ref, q_ref, bias_ref, norm_ref, *refs, n_pages):
    del pt_ref
    k_refs = refs[:n_pages]
    v_refs = refs[n_pages:2 * n_pages]
    o_ref = refs[2 * n_pages]
    page = k_refs[0].shape[2] // N_HEADS
    rows = 2 * SUBLANES
    q = jnp.broadcast_to(q_ref[0] * (HEAD_DIM ** -0.5), (rows, GROUP_W))
    head_of_row = lax.broadcasted_iota(jnp.int32, (rows, GROUP_W), 0)
    head_of_lane = lax.broadcasted_iota(jnp.int32, (rows, GROUP_W), 1) // HEAD_DIM
    q_heads = jnp.where(head_of_row == head_of_lane, q, 0.0).astype(BF16)
    bias = bias_ref[0]
    sum_mat = _suffix_sum_matrix(page)

    def head_rows(refs, h):
        return jnp.concatenate([ref[0, 0, pl.ds(h, page, stride=N_HEADS), :].astype(BF16) for ref in refs], axis=0)

    z = lax.dot_general(q_heads[:, :HEAD_DIM], head_rows(k_refs, 0), _NT, preferred_element_type=F32)
    for h in range(1, N_HEADS):
        z = z + lax.dot_general(q_heads[:, h * HEAD_DIM:(h + 1) * HEAD_DIM], head_rows(k_refs, h), _NT,
                                preferred_element_type=F32)
    z = z + jnp.concatenate([bias] * n_pages, axis=1)
    spend = _stick_spend(z)
    parts, _ = _sb_weights(z, spend, spend, jnp.zeros((rows, page), F32), sum_mat, n_pages, 1)
    a = jnp.concatenate(parts, axis=1).astype(BF16)
    nrm = norm_ref[0]
    for h in range(N_HEADS):
        o = _dot(a, head_rows(v_refs, h))
        o_ref[0, :, h * HEAD_DIM:(h + 1) * HEAD_DIM] = _rms_norm(o[h:h + 1], nrm)


def _sb_sample(q3, cache_k4, cache_v4, page_table, bias_rows, sb_norm3, layer):
    dbsz, n_pages = page_table.shape
    page_rows = cache_k4.shape[2]
    assert page_rows == LANES * N_HEADS

    def page_spec(p):
        return pl.BlockSpec((1, 1, page_rows, HEAD_DIM), lambda b, pt: (layer, pt[b, p], 0, 0))

    grid_spec = pltpu.PrefetchScalarGridSpec(
        num_scalar_prefetch=1,
        grid=(dbsz,),
        in_specs=[
            pl.BlockSpec((1, 1, GROUP_W), lambda b, pt: (b, 0, 0)),
            pl.BlockSpec((1, 2 * SUBLANES, LANES), lambda b, pt: (layer, 0, 0)),
            pl.BlockSpec((1, 1, HEAD_DIM), lambda b, pt: (layer, 0, 0)),
        ] + [page_spec(p) for p in range(n_pages)] * 2,
        out_specs=pl.BlockSpec((1, 1, GROUP_W), lambda b, pt: (b, 0, 0)),
    )
    out = pl.pallas_call(
        functools.partial(_sb_sample_kernel, n_pages=n_pages),
        grid_spec=grid_spec,
        out_shape=jax.ShapeDtypeStruct((dbsz, 1, GROUP_W), F32),
        compiler_params=_params(("parallel",)),
        name="sb_sample",
    )(page_table, q3, bias_rows, sb_norm3, *([cache_k4] * n_pages), *([cache_v4] * n_pages))
    return out.reshape(dbsz, GROUP_W)


def _outproj_kernel(oa_ref, ob_ref, w_ref, x_ref, g_ref, b_ref, *rest, alpha):
    o_ref, ob16_ref = rest[-2:]
    y = _dot(oa_ref[...].astype(BF16), w_ref[0, :GROUP_W, :]) + _dot(ob_ref[...].astype(BF16), w_ref[0, GROUP_W:, :])
    xn = _layer_norm(alpha * x_ref[...] + y, g_ref[0], b_ref[0])
    o_ref[...] = xn
    ob16_ref[...] = xn.astype(BF16)


def _outproj(oa, ob, w_out_b, x, ln_g3, ln_b3, layer, alpha, row_off, into=None):
    t_all, d = x.shape
    n = oa.shape[0]
    tm = _pick_tile(n, 512, 16)
    assert row_off % tm == 0
    off = row_off // tm
    in_specs = [
        pl.BlockSpec((tm, GROUP_W), lambda i: (i, 0)),
        pl.BlockSpec((tm, GROUP_W), lambda i: (i, 0)),
        pl.BlockSpec((1, 2 * GROUP_W, d), lambda i: (layer, 0, 0)),
        pl.BlockSpec((tm, d), lambda i: (off + i, 0)),
        pl.BlockSpec((1, 1, d), lambda i: (layer, 0, 0)),
        pl.BlockSpec((1, 1, d), lambda i: (layer, 0, 0)),
    ]
    args = [oa, ob, w_out_b, x, ln_g3, ln_b3]
    aliases = {}
    if into is not None:
        in_specs += [pl.BlockSpec(memory_space=pl.ANY)] * 2
        aliases = {len(args): 0, len(args) + 1: 1}
        args += list(into)
    return pl.pallas_call(
        functools.partial(_outproj_kernel, alpha=alpha),
        grid=(n // tm,),
        in_specs=in_specs,
        out_specs=[pl.BlockSpec((tm, d), lambda i: (off + i, 0))] * 2,
        out_shape=[jax.ShapeDtypeStruct((t_all, d), F32), jax.ShapeDtypeStruct((t_all, d), BF16)],
        input_output_aliases=aliases,
        compiler_params=_params(("parallel",)),
        name="out_proj_ln",
    )(*args)


def _router_kernel(x_ref, w_ref, o_ref, *, n_experts):
    logits = lax.dot_general(x_ref[...], w_ref[0], (((1,), (0,)), ((), ())), precision=HIGHEST,
                             preferred_element_type=F32)
    lane = lax.broadcasted_iota(jnp.int32, logits.shape, 1).astype(F32)
    neg = jnp.float32(-jnp.inf)
    logits = jnp.where(lane < n_experts, logits, neg)
    m1 = jnp.max(logits, axis=1, keepdims=True)
    i1 = jnp.min(jnp.where(logits == m1, lane, float(LANES)), axis=1, keepdims=True)
    rest = jnp.where(lane == i1, neg, logits)
    m2 = jnp.max(rest, axis=1, keepdims=True)
    i2 = jnp.min(jnp.where(rest == m2, lane, float(LANES)), axis=1, keepdims=True)
    e2 = jnp.exp(m2 - m1)
    denom = 1.0 + e2
    o_ref[...] = jnp.where(lane == i1, 1.0 / denom, 0.0) + jnp.where(lane == i2, e2 / denom, 0.0)


def _router(x, router_pad, layer_idx, n_experts):
    t, d = x.shape
    tm = _pick_tile(t, 1040, 8)
    return pl.pallas_call(
        functools.partial(_router_kernel, n_experts=n_experts),
        grid=(t // tm,),
        in_specs=[
            pl.BlockSpec((tm, d), lambda i: (i, 0)),
            pl.BlockSpec((1, d, LANES), lambda i: (layer_idx, 0, 0)),
        ],
        out_specs=pl.BlockSpec((tm, LANES), lambda i: (i, 0)),
        out_shape=jax.ShapeDtypeStruct((t, LANES), F32),
        compiler_params=_params(("parallel",)),
        name="moe_router",
    )(x, router_pad)


def _ffn_kernel(xb_ref, gate_ref, wg_ref, wu_ref, wd_ref, o_ref):
    e = pl.program_id(1)
    j = pl.program_id(2)
    xb = xb_ref[...]
    g = _dot(xb, wg_ref[0].astype(BF16))
    u = _dot(xb, wu_ref[0].astype(BF16))
    gates = gate_ref[...]
    lane = lax.broadcasted_iota(jnp.int32, gates.shape, 1)
    gate_e = jnp.sum(jnp.where(lane == e, gates, 0.0), axis=1, keepdims=True)
    hidden = (_silu(g) * u * gate_e).astype(BF16)
    contrib = _dot(hidden, wd_ref[0].astype(BF16))
    first = jnp.logical_and(e == 0, j == 0)

    @pl.when(first)
    def _():
        o_ref[...] = contrib

    @pl.when(jnp.logical_not(first))
    def _():
        o_ref[...] += contrib


def _ffn(xb, gates, wg, wu, wd):
    t, d = xb.shape
    n_e, _, f = wg.shape
    tm = _pick_tile(t, 832, 16)
    tf = _pick_tile(f, 512, LANES)
    return pl.pallas_call(
        _ffn_kernel,
        grid=(t // tm, n_e, f // tf),
        in_specs=[
            pl.BlockSpec((tm, d), lambda i, e, j: (i, 0)),
            pl.BlockSpec((tm, LANES), lambda i, e, j: (i, 0)),
            pl.BlockSpec((1, d, tf), lambda i, e, j: (e, 0, j)),
            pl.BlockSpec((1, d, tf), lambda i, e, j: (e, 0, j)),
            pl.BlockSpec((1, tf, d), lambda i, e, j: (e, j, 0)),
        ],
        out_specs=pl.BlockSpec((tm, d), lambda i, e, j: (i, 0)),
        out_shape=jax.ShapeDtypeStruct((t, d), F32),
        compiler_params=_params(("parallel", "arbitrary", "arbitrary")),
        name="ffn",
    )(xb, gates, wg, wu, wd)


def _final_kernel(x_ref, f_ref, p_ref, wg_ref, wp_ref, g_ref, b_ref, o_ref, ob16_ref, *, alpha):
    x = x_ref[...]
    gate = jax.nn.sigmoid(_dot(x.astype(BF16), wg_ref[0]))
    proj = _dot(p_ref[0].astype(BF16), wp_ref[0].astype(BF16))
    xn = _layer_norm(alpha * x + f_ref[...] + gate * proj, g_ref[0], b_ref[0])
    o_ref[...] = xn
    ob16_ref[...] = xn.astype(BF16)


def _final(x, f, p_all, ple_wg_b, ple_wp, ln_g3, ln_b3, layer, alpha):
    t, d = x.shape
    pd = p_all.shape[2]
    tm = _pick_tile(t, 320, 16)
    return pl.pallas_call(
        functools.partial(_final_kernel, alpha=alpha),
        grid=(t // tm,),
        in_specs=[
            pl.BlockSpec((tm, d), lambda i: (i, 0)),
            pl.BlockSpec((tm, d), lambda i: (i, 0)),
            pl.BlockSpec((1, tm, pd), lambda i: (layer, i, 0)),
            pl.BlockSpec((1, d, d), lambda i: (layer, 0, 0)),
            pl.BlockSpec((1, pd, d), lambda i: (layer, 0, 0)),
            pl.BlockSpec((1, 1, d), lambda i: (layer, 0, 0)),
            pl.BlockSpec((1, 1, d), lambda i: (layer, 0, 0)),
        ],
        out_specs=[pl.BlockSpec((tm, d), lambda i: (i, 0))] * 2,
        out_shape=[jax.ShapeDtypeStruct((t, d), F32), jax.ShapeDtypeStruct((t, d), BF16)],
        compiler_params=_params(("parallel",)),
        name="ple_ln",
    )(x, f, p_all, ple_wg_b, ple_wp, ln_g3, ln_b3)


def kernel(x_prompt, x_sample, cache_k, cache_v, state_conv, state_delta, page_table, p_prompt, p_sample, w_in, conv_w, a_log, dt_bias, gdn_norm, sb_norm, sb_bias, w_out, ln1_g, ln1_b, ln2_g, ln2_b, ffn_w_gate, ffn_w_up, ffn_w_down, moe_router, moe_w_gate, moe_w_up, moe_w_down, ple_w_gate, ple_w_proj):
    depth, d, in_dim = w_in.shape
    bsz, seq, _ = x_prompt.shape
    dbsz = x_sample.shape[0]
    assert x_sample.shape[1] == 1 and seq % HEAD_DIM == 0
    assert in_dim == 7 * GROUP_W + 2 * N_HEADS and w_out.shape[1] == 2 * GROUP_W
    tp = bsz * seq
    n_experts = moe_router.shape[2]
    alpha = (2 * depth) ** 0.25

    x = jnp.concatenate([x_prompt.reshape(tp, d), x_sample.reshape(dbsz, d)], axis=0)
    xb = x.astype(BF16)
    p_all = jnp.concatenate([p_prompt.reshape(depth, tp, -1), p_sample.reshape(depth, dbsz, -1)], axis=1)

    gate_lo = 4 * GROUP_W
    sb_lo = gate_lo + 2 * N_HEADS
    nb = 3 * GROUP_W + 2 * 256
    w_a = w_in[:, :, :gate_lo].astype(BF16)
    w_b = jnp.concatenate(
        [w_in[:, :, sb_lo:], w_in[:, :, gate_lo:sb_lo], jnp.zeros((depth, d, nb - 3 * GROUP_W - 2 * N_HEADS), F32)],
        axis=-1).astype(BF16)
    w_out_b = w_out.astype(BF16)
    ple_wg_b = ple_w_gate.astype(BF16)
    router_pad = jnp.pad(moe_router, ((0, 0), (0, 0), (0, LANES - n_experts)))
    bias_rows = jnp.broadcast_to(
        jnp.pad(sb_bias, ((0, 0), (0, 2 * SUBLANES - N_HEADS)))[:, :, None], (depth, 2 * SUBLANES, LANES))
    sc_t = jnp.swapaxes(state_conv, 1, 2)
    cache_k4 = cache_k.reshape(cache_k.shape[:2] + (-1, HEAD_DIM))
    cache_v4 = cache_v.reshape(cache_v.shape[:2] + (-1, HEAD_DIM))
    r3 = lambda a: a.reshape(depth, 1, a.shape[-1])
    gdn_norm3, sb_norm3 = r3(gdn_norm), r3(sb_norm)
    ln1_g3, ln1_b3, ln2_g3, ln2_b3 = r3(ln1_g), r3(ln1_b), r3(ln2_g), r3(ln2_b)
    ones_gate = jnp.ones((tp + dbsz, LANES), F32)

    conv_p, delta_p, k_p, v_p, conv_s, delta_s, k_s, v_s = ([] for _ in range(8))
    for l in range(depth):
        pa = _matmul(xb, w_a, l, tn=2048)
        pb = _matmul(xb, w_b, l, tn=nb // 2)

        oa_p, s_p = _gdn_prompt(pa, pb, conv_w, a_log, dt_bias, gdn_norm3, l, bsz, seq)
        ob_p = _sb_prompt(pb, sb_bias, sb_norm3, l, bsz, seq)
        oa_s, nc_s, s_s = _gdn_sample(pa, pb, sc_t, state_delta, conv_w, a_log, dt_bias, gdn_norm3, l, tp, dbsz)
        q_s = pb[tp:, :GROUP_W].reshape(dbsz, 1, GROUP_W)
        ob_s = _sb_sample(q_s, cache_k4, cache_v4, page_table, bias_rows, sb_norm3, l)

        bufs = _outproj(oa_p, ob_p, w_out_b, x, ln1_g3, ln1_b3, l, alpha, 0)
        x1, x1b = _outproj(oa_s, ob_s, w_out_b, x, ln1_g3, ln1_b3, l, alpha, tp, into=bufs)

        if l % 2 == 0:
            i = l // 2
            f = _ffn(x1b, ones_gate, ffn_w_gate[i:i + 1], ffn_w_up[i:i + 1], ffn_w_down[i:i + 1])
        else:
            i = l // 2
            gates = _router(x1, router_pad, i, n_experts)
            f = _ffn(x1b, gates, moe_w_gate[i], moe_w_up[i], moe_w_down[i])
        x, xb = _final(x1, f, p_all, ple_wg_b, ple_w_proj, ln2_g3, ln2_b3, l, alpha)

        pa_p = pa[:tp, :3 * GROUP_W].reshape(bsz, seq, 3 * GROUP_W)
        conv_p.append(pa_p[:, seq - (CONV_K - 1):])
        delta_p.append(s_p)
        k_p.append(pb[:tp, GROUP_W:2 * GROUP_W].reshape(bsz, seq, N_HEADS, HEAD_DIM))
        v_p.append(pb[:tp, 2 * GROUP_W:3 * GROUP_W].reshape(bsz, seq, N_HEADS, HEAD_DIM))
        conv_s.append(jnp.swapaxes(nc_s, 0, 1))
        delta_s.append(s_s[0])
        k_s.append(pb[tp:, GROUP_W:2 * GROUP_W].reshape(dbsz, 1, N_HEADS, HEAD_DIM))
        v_s.append(pb[tp:, 2 * GROUP_W:3 * GROUP_W].reshape(dbsz, 1, N_HEADS, HEAD_DIM))

    y_prompt = x[:tp].reshape(bsz, seq, d)
    y_sample = x[tp:].reshape(dbsz, 1, d)
    st = jnp.stack
    return (y_prompt, y_sample, st(conv_p), st(delta_p), st(k_p), st(v_p), st(conv_s), st(delta_s), st(k_s), st(v_s))
```

```python
import functools
import math

import jax
import jax.numpy as jnp
from jax import lax
from jax.experimental import pallas as pl
from jax.experimental.pallas import tpu as pltpu

F32 = jnp.float32
BF16 = jnp.bfloat16
HIGHEST = lax.Precision.HIGHEST

HEAD_DIM = 128
N_HEADS = 8
GROUP_W = N_HEADS * HEAD_DIM
CONV_K = 4
GDN_BASE_BLOCK = 16
LN_EPS = 1e-5
RMS_EPS = 1e-6
L2_EPS = 1e-6
SUBLANES = 8
LANES = 128
VMEM_BYTES_V7X = 64 * 1024 * 1024
VMEM_LIMIT = VMEM_BYTES_V7X * 7 // 8

_NT = (((1,), (1,)), ((), ()))
_BNN = (((2,), (1,)), ((0,), (0,)))
_BNT = (((2,), (2,)), ((0,), (0,)))
_BTN = (((1,), (1,)), ((0,), (0,)))


def _pick_tile(n, target, mult):
    best = None
    for t in range(mult, min(n, target) + 1, mult):
        if n % t == 0:
            best = t
    assert best is not None, (n, target, mult)
    return best


def _params(sem):
    return pltpu.CompilerParams(dimension_semantics=sem, vmem_limit_bytes=VMEM_LIMIT)


def _silu(x):
    return x * jax.nn.sigmoid(x)


def _softplus(x):
    return jnp.maximum(x, 0.0) + jnp.log1p(jnp.exp(-jnp.abs(x)))


def _dot(a, b):
    return jnp.dot(a, b, preferred_element_type=F32)


def _split_bf16(a):
    hi = a.astype(BF16)
    return hi, (a - hi.astype(F32)).astype(BF16)


def _dot1(a, b, dims):
    return lax.dot_general(a.astype(BF16), b.astype(BF16), dims, preferred_element_type=F32)


def _layer_norm(r, g, b):
    mu = jnp.mean(r, axis=-1, keepdims=True)
    c = r - mu
    var = jnp.mean(c * c, axis=-1, keepdims=True)
    return c * lax.rsqrt(var + LN_EPS) * g + b


def _rms_norm(x, g):
    return x * lax.rsqrt(jnp.mean(x * x, axis=-1, keepdims=True) + RMS_EPS) * g


def _l2norm(x):
    return x * lax.rsqrt(jnp.sum(x * x, axis=-1, keepdims=True) + L2_EPS)


def _matmul_kernel(x_ref, w_ref, o_ref):
    o_ref[...] = _dot(x_ref[...], w_ref[0])


def _matmul(xb, w, layer, tn):
    t, k = xb.shape
    n = w.shape[2]
    tm = _pick_tile(t, 1040, 16)
    return pl.pallas_call(
        _matmul_kernel,
        grid=(t // tm, n // tn),
        in_specs=[
            pl.BlockSpec((tm, k), lambda i, j: (i, 0)),
            pl.BlockSpec((1, k, tn), lambda i, j: (layer, 0, j)),
        ],
        out_specs=pl.BlockSpec((tm, tn), lambda i, j: (i, j)),
        out_shape=jax.ShapeDtypeStruct((t, n), F32),
        compiler_params=_params(("parallel", "arbitrary")),
        name="in_proj",
    )(xb, w)


def _kv_proj_kernel(x_ref, w_ref, *rest):
    kv_ref, k_ref, v_ref = rest[-3:]
    tm = x_ref.shape[0]
    res = _dot(x_ref[...], w_ref[0])
    kv_ref[...] = res
    for h in range(N_HEADS):
        lo = h * HEAD_DIM
        k_ref[0, pl.ds(h, tm, stride=N_HEADS), :] = res[:, lo:lo + HEAD_DIM]
        v_ref[0, pl.ds(h, tm, stride=N_HEADS), :] = res[:, GROUP_W + lo:GROUP_W + lo + HEAD_DIM]


def _kv_proj(xb, w_kv, layer, row_off, n_rows, into=None):
    depth, d, n = w_kv.shape
    tm = _pick_tile(n_rows, 512, 2 * SUBLANES)
    assert row_off % tm == 0 and n == 2 * GROUP_W
    off = row_off // tm
    in_specs = [
        pl.BlockSpec((tm, d), lambda i: (off + i, 0)),
        pl.BlockSpec((1, d, n), lambda i: (layer, 0, 0)),
    ]
    args = [xb, w_kv]
    aliases = {}
    if into is not None:
        in_specs += [pl.BlockSpec(memory_space=pl.ANY)] * 2
        aliases = {2: 1, 3: 2}
        args += list(into)
    cache_shape = jax.ShapeDtypeStruct((depth, n_rows * N_HEADS, HEAD_DIM), F32)
    cache_spec = pl.BlockSpec((1, tm * N_HEADS, HEAD_DIM), lambda i: (layer, i, 0))
    return pl.pallas_call(
        _kv_proj_kernel,
        grid=(n_rows // tm,),
        in_specs=in_specs,
        out_specs=[pl.BlockSpec((tm, n), lambda i: (i, 0)), cache_spec, cache_spec],
        out_shape=[jax.ShapeDtypeStruct((n_rows, n), F32), cache_shape, cache_shape],
        input_output_aliases=aliases,
        compiler_params=_params(("parallel",)),
        name="kv_proj",
    )(*args)


def _gdn_heads(qkv, gt, alog_ref, dtb_ref, layer):
    rows = qkv.shape[0]
    qs, ks, vs, betas, gls = [], [], [], [], []
    for h in range(N_HEADS):
        lo = h * HEAD_DIM
        qs.append(_l2norm(qkv[:, lo:lo + HEAD_DIM]) * (HEAD_DIM ** -0.5))
        ks.append(_l2norm(qkv[:, GROUP_W + lo:GROUP_W + lo + HEAD_DIM]))
        vs.append(qkv[:, 2 * GROUP_W + lo:2 * GROUP_W + lo + HEAD_DIM])
        b_logit = jnp.broadcast_to(gt[:, h:h + 1], (rows, HEAD_DIM))
        a_logit = jnp.broadcast_to(gt[:, N_HEADS + h:N_HEADS + h + 1], (rows, HEAD_DIM))
        betas.append(jax.nn.sigmoid(b_logit))
        rate = jnp.exp(jnp.full((1, HEAD_DIM), alog_ref[layer, h], F32))
        gls.append(-rate * _softplus(a_logit + dtb_ref[layer, h]))
    return tuple(jnp.stack(a, axis=0) for a in (qs, ks, vs, betas, gls))


def _gdn_prompt_kernel(alog_ref, dtb_ref, pa_ref, g_ref, cw_ref, norm_ref, o_ref, s_out_ref, s_ref, xp_ref, *,
                       layer, n_chunks):
    t = pl.program_id(1)
    c = pa_ref.shape[0]
    w3 = 3 * GROUP_W

    @pl.when(t == 0)
    def _():
        s_ref[...] = jnp.zeros_like(s_ref)
        xp_ref[0:SUBLANES, :] = jnp.zeros((SUBLANES, w3), F32)

    x = pa_ref[:, :w3]
    xp_ref[SUBLANES:SUBLANES + c, :] = x
    cw = cw_ref[0]
    acc = cw[CONV_K - 1:CONV_K, :] * x
    for back in range(1, CONV_K):
        acc = acc + cw[CONV_K - 1 - back:CONV_K - back, :] * xp_ref[SUBLANES - back:SUBLANES - back + c, :]
    xp_ref[0:SUBLANES, :] = x[c - SUBLANES:, :]
    qkv = _silu(acc)

    q, k, v, beta, gl = _gdn_heads(qkv, g_ref[...], alog_ref, dtb_ref, layer)

    row = lax.broadcasted_iota(jnp.int32, (c, c), 0)
    col = lax.broadcasted_iota(jnp.int32, (c, c), 1)
    incl = (col <= row)[None]
    strict = (col < row)[None]
    tri_incl = jnp.broadcast_to((col <= row).astype(BF16)[None], (N_HEADS, c, c))
    eye = (row == col).astype(F32)[None]

    gl_hi, gl_mid = _split_bf16(gl)
    gl_lo = (gl - gl_hi.astype(F32) - gl_mid.astype(F32)).astype(BF16)
    cdot = functools.partial(lax.dot_general, tri_incl, dimension_numbers=_BNN, preferred_element_type=F32)
    gc = cdot(gl_hi) + cdot(gl_mid) + cdot(gl_lo)
    gc_row = jnp.stack([gc[h].T for h in range(N_HEADS)], axis=0)
    decay = jnp.where(incl, jnp.exp(jnp.where(incl, gc - gc_row, 0.0)), 0.0)
    kb = k * beta
    vb = v * beta
    neg_a = -jnp.where(strict, _dot1(kb, k, _BNT) * decay, 0.0)
    same_block = lambda n: ((row // n) == (col // n))[None]
    power = jnp.where(same_block(GDN_BASE_BLOCK), neg_a, 0.0)
    inv = eye + power
    for _ in range(int(math.log2(GDN_BASE_BLOCK)) - 1):
        power = _dot1(power, power, _BNN)
        inv = inv + _dot1(inv, power, _BNN)
    blk = GDN_BASE_BLOCK
    while blk < c:
        below = jnp.where(jnp.logical_and(same_block(2 * blk), jnp.logical_not(same_block(blk))), neg_a, 0.0)
        inv = inv + _dot1(inv, _dot1(below, inv, _BNN), _BNN)
        blk *= 2
    eg = jnp.exp(gc)
    u = _dot1(inv, vb, _BNN)
    w = _dot1(inv, kb * eg, _BNN)
    a_intra = _dot1(q, k, _BNT) * decay
    g_last = gc[:, c - 1:c, :]
    k_dec = k * jnp.exp(g_last - gc)

    s = s_ref[...]
    v_new = u - _dot1(w, s, _BNN)
    o = _dot1(q * eg, s, _BNN) + _dot1(a_intra, v_new, _BNN)
    s_new = s * jnp.exp(g_last) + _dot1(k_dec, v_new, _BTN)
    s_ref[...] = s_new

    nrm = norm_ref[0]
    for h in range(N_HEADS):
        lo = h * HEAD_DIM
        z = pa_ref[:, w3 + lo:w3 + lo + HEAD_DIM]
        o_ref[:, lo:lo + HEAD_DIM] = _rms_norm(o[h], nrm) * _silu(z)

    @pl.when(t == n_chunks - 1)
    def _():
        s_out_ref[0] = s_new


def _gdn_prompt(pa, pq, conv_w, a_log, dt_bias, gdn_norm3, layer, bsz, seq):
    c = HEAD_DIM
    nt = seq // c
    gate_blk = GROUP_W // LANES
    return pl.pallas_call(
        functools.partial(_gdn_prompt_kernel, layer=layer, n_chunks=nt),
        grid=(bsz, nt),
        in_specs=[
            pl.BlockSpec(memory_space=pltpu.SMEM),
            pl.BlockSpec(memory_space=pltpu.SMEM),
            pl.BlockSpec((c, 4 * GROUP_W), lambda b, t: (b * nt + t, 0)),
            pl.BlockSpec((c, LANES), lambda b, t: (b * nt + t, gate_blk)),
            pl.BlockSpec((1, CONV_K, 3 * GROUP_W), lambda b, t: (layer, 0, 0)),
            pl.BlockSpec((1, 1, HEAD_DIM), lambda b, t: (layer, 0, 0)),
        ],
        out_specs=[
            pl.BlockSpec((c, GROUP_W), lambda b, t: (b * nt + t, 0)),
            pl.BlockSpec((1, N_HEADS, HEAD_DIM, HEAD_DIM), lambda b, t: (b, 0, 0, 0)),
        ],
        out_shape=[
            jax.ShapeDtypeStruct((bsz * seq, GROUP_W), F32),
            jax.ShapeDtypeStruct((bsz, N_HEADS, HEAD_DIM, HEAD_DIM), F32),
        ],
        scratch_shapes=[
            pltpu.VMEM((N_HEADS, HEAD_DIM, HEAD_DIM), F32),
            pltpu.VMEM((SUBLANES + c, 3 * GROUP_W), F32),
        ],
        compiler_params=_params(("parallel", "arbitrary")),
        name="gdn_prompt",
    )(a_log, dt_bias, pa, pq, conv_w, gdn_norm3)


def _gdn_sample_kernel(alog_ref, dtb_ref, pa_ref, g_ref, sc_ref, cw_ref, norm_ref, s_ref, *rest, layer):
    o_ref, nc_ref, s_out_ref = rest[-3:]
    bb = pa_ref.shape[0]
    w3 = 3 * GROUP_W
    x = pa_ref[:, :w3]
    cw = cw_ref[0]
    acc = cw[CONV_K - 1:CONV_K, :] * x
    for i in range(CONV_K - 1):
        acc = acc + cw[i:i + 1, :] * sc_ref[0, i]
    for i in range(CONV_K - 2):
        nc_ref[i] = sc_ref[0, i + 1]
    nc_ref[CONV_K - 2] = x
    qkv = _silu(acc)

    q, k, v, beta, gl = _gdn_heads(qkv, g_ref[...], alog_ref, dtb_ref, layer)
    eg = jnp.exp(gl)
    kq_t = jnp.concatenate([k.reshape(N_HEADS * bb, HEAD_DIM), q.reshape(N_HEADS * bb, HEAD_DIM)], axis=0).T
    nrm = norm_ref[0]
    for h in range(N_HEADS):
        lo = h * HEAD_DIM
        z = pa_ref[:, w3 + lo:w3 + lo + HEAD_DIM]
        for b in range(bb):
            p = h * bb + b
            kcol = kq_t[:, p:p + 1]
            qcol = kq_t[:, N_HEADS * bb + p:N_HEADS * bb + p + 1]
            s = s_ref[0, b, h]
            ks = jnp.sum(kcol * s, axis=0, keepdims=True)
            qs = jnp.sum(qcol * s, axis=0, keepdims=True)
            e = eg[h, b:b + 1]
            bt = beta[h, b:b + 1]
            v_new = bt * v[h, b:b + 1] - bt * e * ks
            qk = jnp.sum(q[h, b:b + 1] * k[h, b:b + 1], axis=1, keepdims=True)
            o = e * qs + qk * v_new
            s_out_ref[0, b, h] = s * e + kcol * v_new
            o_ref[b:b + 1, lo:lo + HEAD_DIM] = _rms_norm(o, nrm) * _silu(z[b:b + 1])


def _gdn_sample(pa, pq, sc_t, state_delta, conv_w, a_log, dt_bias, gdn_norm3, layer, t_prompt, dbsz, into=None):
    bb = LANES // (2 * N_HEADS)
    assert dbsz % bb == 0 and t_prompt % bb == 0
    off = t_prompt // bb
    gate_blk = GROUP_W // LANES
    state_spec = pl.BlockSpec((1, bb, N_HEADS, HEAD_DIM, HEAD_DIM), lambda i: (layer, i, 0, 0, 0))
    in_specs = [
        pl.BlockSpec(memory_space=pltpu.SMEM),
        pl.BlockSpec(memory_space=pltpu.SMEM),
        pl.BlockSpec((bb, 4 * GROUP_W), lambda i: (off + i, 0)),
        pl.BlockSpec((bb, LANES), lambda i: (off + i, gate_blk)),
        pl.BlockSpec((1, CONV_K - 1, bb, 3 * GROUP_W), lambda i: (layer, 0, i, 0)),
        pl.BlockSpec((1, CONV_K, 3 * GROUP_W), lambda i: (layer, 0, 0)),
        pl.BlockSpec((1, 1, HEAD_DIM), lambda i: (layer, 0, 0)),
        state_spec,
    ]
    args = [a_log, dt_bias, pa, pq, sc_t, conv_w, gdn_norm3, state_delta]
    aliases = {}
    if into is not None:
        in_specs.append(pl.BlockSpec(memory_space=pl.ANY))
        aliases = {len(args): 2}
        args.append(into)
    return pl.pallas_call(
        functools.partial(_gdn_sample_kernel, layer=layer),
        grid=(dbsz // bb,),
        in_specs=in_specs,
        out_specs=[
            pl.BlockSpec((bb, GROUP_W), lambda i: (i, 0)),
            pl.BlockSpec((CONV_K - 1, bb, 3 * GROUP_W), lambda i: (0, i, 0)),
            state_spec,
        ],
        out_shape=[
            jax.ShapeDtypeStruct((dbsz, GROUP_W), F32),
            jax.ShapeDtypeStruct((CONV_K - 1, dbsz, 3 * GROUP_W), F32),
            jax.ShapeDtypeStruct(state_delta.shape, F32),
        ],
        input_output_aliases=aliases,
        compiler_params=_params(("parallel",)),
        name="gdn_sample",
    )(*args)


def _suffix_sum_matrix(keys):
    row = lax.broadcasted_iota(jnp.int32, (keys, 2 * keys), 0)
    col = lax.broadcasted_iota(jnp.int32, (keys, 2 * keys), 1)
    return jnp.logical_or(row > col, col >= keys).astype(BF16)


def _stick_spend(z):
    return jnp.maximum(z, 0.0) + jnp.log(1.0 + jnp.exp(-jnp.abs(z)))


def _sb_weights(z, spend, spend_masked, r, sum_mat, n_tiles, axis):
    keys = sum_mat.shape[0]
    size = z.shape[axis] // n_tiles

    def tile(m, c):
        return lax.slice_in_dim(m, c * size, (c + 1) * size, axis=axis)

    rows = tile(z, 0).shape[0]
    hi, lo = _split_bf16(spend_masked)
    stack = lambda m: jnp.concatenate([tile(m, c) for c in range(n_tiles)], axis=0)
    sums = _dot(jnp.concatenate([stack(hi), stack(lo)], axis=1), jnp.concatenate([sum_mat, sum_mat], axis=0))
    parts = [None] * n_tiles
    for c in reversed(range(n_tiles)):
        s_c = sums[c * rows:(c + 1) * rows]
        parts[c] = jnp.exp(tile(z, c) - tile(spend, c) - s_c[:, :keys] - r)
        r = r + s_c[:, keys:]
    return parts, r


def _sb_prompt_kernel(bias_ref, q_ref, k_ref, v_ref, norm_ref, o_ref, *, layer, chunk):
    h = pl.program_id(1)
    i = pl.program_id(2)
    tq = q_ref.shape[0]
    n_sub = tq // chunk
    qb = (q_ref[...] * (HEAD_DIM ** -0.5)).astype(BF16)
    bias = bias_ref[layer, h]
    sum_mat = _suffix_sum_matrix(chunk)

    def block(start, r, acc, causal):
        kblk = k_ref[pl.ds(start, tq), :].astype(BF16)
        vblk = v_ref[pl.ds(start, tq), :].astype(BF16)
        z = lax.dot_general(qb, kblk, _NT, preferred_element_type=F32) + bias
        spend = _stick_spend(z)
        masked = spend if causal is None else jnp.where(causal, spend, 0.0)
        parts, r = _sb_weights(z, spend, masked, r, sum_mat, n_sub, 1)
        a = jnp.concatenate(parts, axis=1)
        if causal is not None:
            a = jnp.where(causal, a, 0.0)
        return r, acc + _dot(a.astype(BF16), vblk)

    row = lax.broadcasted_iota(jnp.int32, (tq, tq), 0)
    col = lax.broadcasted_iota(jnp.int32, (tq, tq), 1)
    init = (jnp.zeros((tq, chunk), F32), jnp.zeros((tq, HEAD_DIM), F32))
    carry = block(pl.multiple_of(i * tq, tq), *init, col < row)

    def body(jj, carry):
        return block(pl.multiple_of((i - 1 - jj) * tq, tq), *carry, None)

    _, acc = lax.fori_loop(0, i, body, carry)
    o_ref[...] = _rms_norm(acc, norm_ref[0])


def _sb_prompt(pq, kv, sb_bias, sb_norm3, layer, bsz, seq):
    tq = _pick_tile(seq, 512, LANES)
    nq = seq // tq
    return pl.pallas_call(
        functools.partial(_sb_prompt_kernel, layer=layer, chunk=LANES),
        grid=(bsz, N_HEADS, nq),
        in_specs=[
            pl.BlockSpec(memory_space=pltpu.SMEM),
            pl.BlockSpec((tq, HEAD_DIM), lambda b, h, i: (b * nq + i, h)),
            pl.BlockSpec((seq, HEAD_DIM), lambda b, h, i: (b, h)),
            pl.BlockSpec((seq, HEAD_DIM), lambda b, h, i: (b, N_HEADS + h)),
            pl.BlockSpec((1, 1, HEAD_DIM), lambda b, h, i: (layer, 0, 0)),
        ],
        out_specs=pl.BlockSpec((tq, HEAD_DIM), lambda b, h, i: (b * nq + i, h)),
        out_shape=jax.ShapeDtypeStruct((bsz * seq, GROUP_W), F32),
        compiler_params=_params(("parallel", "parallel", "arbitrary")),
        name="sb_prompt",
    )(sb_bias, pq, kv, kv, sb_norm3)


def _sb_sample_kernel(pt_ref, q_ref, bias_ref, norm_ref, *refs, n_pages):
    del pt_ref
    k_refs = refs[:n_pages]
    v_refs = refs[n_pages:2 * n_pages]
    o_ref = refs[2 * n_pages]
    page = k_refs[0].shape[2] // N_HEADS
    rows = 2 * SUBLANES
    q = jnp.broadcast_to(q_ref[0] * (HEAD_DIM ** -0.5), (rows, GROUP_W))
    head_of_row = lax.broadcasted_iota(jnp.int32, (rows, GROUP_W), 0)
    head_of_lane = lax.broadcasted_iota(jnp.int32, (rows, GROUP_W), 1) // HEAD_DIM
    q_heads = jnp.where(head_of_row == head_of_lane, q, 0.0).astype(BF16)
    bias = bias_ref[0]
    sum_mat = _suffix_sum_matrix(page)

    def head_rows(refs, h):
        return jnp.concatenate([ref[0, 0, pl.ds(h, page, stride=N_HEADS), :].astype(BF16) for ref in refs], axis=0)

    z = lax.dot_general(q_heads[:, :HEAD_DIM], head_rows(k_refs, 0), _NT, preferred_element_type=F32)
    for h in range(1, N_HEADS):
        z = z + lax.dot_general(q_heads[:, h * HEAD_DIM:(h + 1) * HEAD_DIM], head_rows(k_refs, h), _NT,
                                preferred_element_type=F32)
    z = z + jnp.concatenate([bias] * n_pages, axis=1)
    spend = _stick_spend(z)
    parts, _ = _sb_weights(z, spend, spend, jnp.zeros((rows, page), F32), sum_mat, n_pages, 1)
    a = jnp.concatenate(parts, axis=1).astype(BF16)
    nrm = norm_ref[0]
    for h in range(N_HEADS):
        o = _dot(a, head_rows(v_refs, h))
        o_ref[0, :, h * HEAD_DIM:(h + 1) * HEAD_DIM] = _rms_norm(o[h:h + 1], nrm)


def _sb_sample(q3, cache_k4, cache_v4, page_table, bias_rows, sb_norm3, layer):
    dbsz, n_pages = page_table.shape
    page_rows = cache_k4.shape[2]
    assert page_rows == LANES * N_HEADS

    def page_spec(p):
        return pl.BlockSpec((1, 1, page_rows, HEAD_DIM), lambda b, pt: (layer, pt[b, p], 0, 0))

    grid_spec = pltpu.PrefetchScalarGridSpec(
        num_scalar_prefetch=1,
        grid=(dbsz,),
        in_specs=[
            pl.BlockSpec((1, 1, GROUP_W), lambda b, pt: (b, 0, 0)),
            pl.BlockSpec((1, 2 * SUBLANES, LANES), lambda b, pt: (layer, 0, 0)),
            pl.BlockSpec((1, 1, HEAD_DIM), lambda b, pt: (layer, 0, 0)),
        ] + [page_spec(p) for p in range(n_pages)] * 2,
        out_specs=pl.BlockSpec((1, 1, GROUP_W), lambda b, pt: (b, 0, 0)),
    )
    out = pl.pallas_call(
        functools.partial(_sb_sample_kernel, n_pages=n_pages),
        grid_spec=grid_spec,
        out_shape=jax.ShapeDtypeStruct((dbsz, 1, GROUP_W), F32),
        compiler_params=_params(("parallel",)),
        name="sb_sample",
    )(page_table, q3, bias_rows, sb_norm3, *([cache_k4] * n_pages), *([cache_v4] * n_pages))
    return out.reshape(dbsz, GROUP_W)


def _outproj_kernel(oa_ref, ob_ref, w_ref, x_ref, g_ref, b_ref, *rest, alpha):
    o_ref, ob16_ref = rest[-2:]
    y = _dot(oa_ref[...].astype(BF16), w_ref[0, :GROUP_W, :]) + _dot(ob_ref[...].astype(BF16), w_ref[0, GROUP_W:, :])
    xn = _layer_norm(alpha * x_ref[...] + y, g_ref[0], b_ref[0])
    o_ref[...] = xn
    ob16_ref[...] = xn.astype(BF16)


def _outproj(oa, ob, w_out_b, x, ln_g3, ln_b3, layer, alpha, row_off, into=None):
    t_all, d = x.shape
    n = oa.shape[0]
    tm = _pick_tile(n, 512, 16)
    assert row_off % tm == 0
    off = row_off // tm
    in_specs = [
        pl.BlockSpec((tm, GROUP_W), lambda i: (i, 0)),
        pl.BlockSpec((tm, GROUP_W), lambda i: (i, 0)),
        pl.BlockSpec((1, 2 * GROUP_W, d), lambda i: (layer, 0, 0)),
        pl.BlockSpec((tm, d), lambda i: (off + i, 0)),
        pl.BlockSpec((1, 1, d), lambda i: (layer, 0, 0)),
        pl.BlockSpec((1, 1, d), lambda i: (layer, 0, 0)),
    ]
    args = [oa, ob, w_out_b, x, ln_g3, ln_b3]
    aliases = {}
    if into is not None:
        in_specs += [pl.BlockSpec(memory_space=pl.ANY)] * 2
        aliases = {len(args): 0, len(args) + 1: 1}
        args += list(into)
    return pl.pallas_call(
        functools.partial(_outproj_kernel, alpha=alpha),
        grid=(n // tm,),
        in_specs=in_specs,
        out_specs=[pl.BlockSpec((tm, d), lambda i: (off + i, 0))] * 2,
        out_shape=[jax.ShapeDtypeStruct((t_all, d), F32), jax.ShapeDtypeStruct((t_all, d), BF16)],
        input_output_aliases=aliases,
        compiler_params=_params(("parallel",)),
        name="out_proj_ln",
    )(*args)


def _router_kernel(x_ref, w_ref, o_ref, *, n_experts):
    logits = lax.dot_general(x_ref[...], w_ref[0], (((1,), (0,)), ((), ())), precision=HIGHEST,
                             preferred_element_type=F32)
    lane = lax.broadcasted_iota(jnp.int32, logits.shape, 1).astype(F32)
    neg = jnp.float32(-jnp.inf)
    logits = jnp.where(lane < n_experts, logits, neg)
    m1 = jnp.max(logits, axis=1, keepdims=True)
    i1 = jnp.min(jnp.where(logits == m1, lane, float(LANES)), axis=1, keepdims=True)
    rest = jnp.where(lane == i1, neg, logits)
    m2 = jnp.max(rest, axis=1, keepdims=True)
    i2 = jnp.min(jnp.where(rest == m2, lane, float(LANES)), axis=1, keepdims=True)
    e2 = jnp.exp(m2 - m1)
    denom = 1.0 + e2
    o_ref[...] = jnp.where(lane == i1, 1.0 / denom, 0.0) + jnp.where(lane == i2, e2 / denom, 0.0)


def _router(x, router_pad, layer_idx, n_experts):
    t, d = x.shape
    tm = _pick_tile(t, 1040, 8)
    return pl.pallas_call(
        functools.partial(_router_kernel, n_experts=n_experts),
        grid=(t // tm,),
        in_specs=[
            pl.BlockSpec((tm, d), lambda i: (i, 0)),
            pl.BlockSpec((1, d, LANES), lambda i: (layer_idx, 0, 0)),
        ],
        out_specs=pl.BlockSpec((tm, LANES), lambda i: (i, 0)),
        out_shape=jax.ShapeDtypeStruct((t, LANES), F32),
        compiler_params=_params(("parallel",)),
        name="moe_router",
    )(x, router_pad)


def _ffn_kernel(xb_ref, gate_ref, wg_ref, wu_ref, wd_ref, o_ref):
    e = pl.program_id(1)
    j = pl.program_id(2)

    @pl.when(jnp.logical_and(e == 0, j == 0))
    def _():
        o_ref[...] = jnp.zeros_like(o_ref)

    xb = xb_ref[...]
    g = _dot(xb, wg_ref[0].astype(BF16))
    u = _dot(xb, wu_ref[0].astype(BF16))
    gates = gate_ref[...]
    lane = lax.broadcasted_iota(jnp.int32, gates.shape, 1)
    gate_e = jnp.sum(jnp.where(lane == e, gates, 0.0), axis=1, keepdims=True)
    hidden = (_silu(g) * u * gate_e).astype(BF16)
    o_ref[...] += _dot(hidden, wd_ref[0].astype(BF16))


def _ffn(xb, gates, wg, wu, wd):
    t, d = xb.shape
    n_e, _, f = wg.shape
    tm = _pick_tile(t, 832, 16)
    tf = _pick_tile(f, 512, LANES)
    return pl.pallas_call(
        _ffn_kernel,
        grid=(t // tm, n_e, f // tf),
        in_specs=[
            pl.BlockSpec((tm, d), lambda i, e, j: (i, 0)),
            pl.BlockSpec((tm, LANES), lambda i, e, j: (i, 0)),
            pl.BlockSpec((1, d, tf), lambda i, e, j: (e, 0, j)),
            pl.BlockSpec((1, d, tf), lambda i, e, j: (e, 0, j)),
            pl.BlockSpec((1, tf, d), lambda i, e, j: (e, j, 0)),
        ],
        out_specs=pl.BlockSpec((tm, d), lambda i, e, j: (i, 0)),
        out_shape=jax.ShapeDtypeStruct((t, d), F32),
        compiler_params=_params(("parallel", "arbitrary", "arbitrary")),
        name="ffn",
    )(xb, gates, wg, wu, wd)


def _final_kernel(x_ref, f_ref, p_ref, wg_ref, wp_ref, g_ref, b_ref, o_ref, *maybe_bf16_ref, alpha):
    x = x_ref[...]
    gate = jax.nn.sigmoid(_dot(x.astype(BF16), wg_ref[0]))
    proj = _dot(p_ref[0].astype(BF16), wp_ref[0].astype(BF16))
    xn = _layer_norm(alpha * x + f_ref[...] + gate * proj, g_ref[0], b_ref[0])
    o_ref[...] = xn
    for ref in maybe_bf16_ref:
        ref[...] = xn.astype(BF16)


def _final(x, f, p_all, ple_wg_b, ple_wp, ln_g3, ln_b3, layer, alpha, row_off=0, n_rows=None, with_bf16=True):
    t, d = x.shape
    n_rows = t if n_rows is None else n_rows
    pd = p_all.shape[2]
    tm = _pick_tile(math.gcd(n_rows, row_off) if row_off else n_rows, 320, 16)
    off = row_off // tm
    row_spec = pl.BlockSpec((tm, d), lambda i: (off + i, 0))
    out_shape = [jax.ShapeDtypeStruct((n_rows, d), F32)] + ([jax.ShapeDtypeStruct((n_rows, d), BF16)] * with_bf16)
    return pl.pallas_call(
        functools.partial(_final_kernel, alpha=alpha),
        grid=(n_rows // tm,),
        in_specs=[
            row_spec,
            row_spec,
            pl.BlockSpec((1, tm, pd), lambda i: (layer, off + i, 0)),
            pl.BlockSpec((1, d, d), lambda i: (layer, 0, 0)),
            pl.BlockSpec((1, pd, d), lambda i: (layer, 0, 0)),
            pl.BlockSpec((1, 1, d), lambda i: (layer, 0, 0)),
            pl.BlockSpec((1, 1, d), lambda i: (layer, 0, 0)),
        ],
        out_specs=[pl.BlockSpec((tm, d), lambda i: (i, 0))] * len(out_shape),
        out_shape=out_shape,
        compiler_params=_params(("parallel",)),
        name="ple_ln",
    )(x, f, p_all, ple_wg_b, ple_wp, ln_g3, ln_b3)


def kernel(x_prompt, x_sample, cache_k, cache_v, state_conv, state_delta, page_table, p_prompt, p_sample, w_in, conv_w, a_log, dt_bias, gdn_norm, sb_norm, sb_bias, w_out, ln1_g, ln1_b, ln2_g, ln2_b, ffn_w_gate, ffn_w_up, ffn_w_down, moe_router, moe_w_gate, moe_w_up, moe_w_down, ple_w_gate, ple_w_proj):
    depth, d, in_dim = w_in.shape
    bsz, seq, _ = x_prompt.shape
    dbsz = x_sample.shape[0]
    assert x_sample.shape[1] == 1 and seq % HEAD_DIM == 0
    assert in_dim == 7 * GROUP_W + 2 * N_HEADS and w_out.shape[1] == 2 * GROUP_W
    tp = bsz * seq
    n_experts = moe_router.shape[2]
    alpha = (2 * depth) ** 0.25

    x = jnp.concatenate([x_prompt.reshape(tp, d), x_sample.reshape(dbsz, d)], axis=0)
    xb = x.astype(BF16)
    p_all = jnp.concatenate([p_prompt.reshape(depth, tp, -1), p_sample.reshape(depth, dbsz, -1)], axis=1)

    gate_lo = 4 * GROUP_W
    sb_lo = gate_lo + 2 * N_HEADS
    nq = GROUP_W + 256
    w_a = w_in[:, :, :gate_lo].astype(BF16)
    w_q = jnp.concatenate(
        [w_in[:, :, sb_lo:sb_lo + GROUP_W], w_in[:, :, gate_lo:sb_lo],
         jnp.zeros((depth, d, nq - GROUP_W - 2 * N_HEADS), F32)], axis=-1).astype(BF16)
    w_kv = w_in[:, :, sb_lo + GROUP_W:].astype(BF16)
    w_out_b = w_out.astype(BF16)
    ple_wg_b = ple_w_gate.astype(BF16)
    router_pad = jnp.pad(moe_router, ((0, 0), (0, 0), (0, LANES - n_experts)))
    bias_rows = jnp.broadcast_to(
        jnp.pad(sb_bias, ((0, 0), (0, 2 * SUBLANES - N_HEADS)))[:, :, None], (depth, 2 * SUBLANES, LANES))
    sc_t = jnp.swapaxes(state_conv, 1, 2)
    cache_k4 = cache_k.reshape(cache_k.shape[:2] + (-1, HEAD_DIM))
    cache_v4 = cache_v.reshape(cache_v.shape[:2] + (-1, HEAD_DIM))
    r3 = lambda a: a.reshape(depth, 1, a.shape[-1])
    gdn_norm3, sb_norm3 = r3(gdn_norm), r3(sb_norm)
    ln1_g3, ln1_b3, ln2_g3, ln2_b3 = r3(ln1_g), r3(ln1_b), r3(ln2_g), r3(ln2_b)
    ones_gate = jnp.ones((tp + dbsz, LANES), F32)

    conv_p, delta_p, conv_s = [], [], []
    kv_cache_p = kv_cache_s = delta_s = None
    for l in range(depth):
        pa = _matmul(xb, w_a, l, tn=2048)
        pq = _matmul(xb, w_q, l, tn=nq)
        kv_p, *kv_cache_p = _kv_proj(xb, w_kv, l, 0, tp, into=kv_cache_p)
        _, *kv_cache_s = _kv_proj(xb, w_kv, l, tp, dbsz, into=kv_cache_s)

        oa_p, s_p = _gdn_prompt(pa, pq, conv_w, a_log, dt_bias, gdn_norm3, l, bsz, seq)
        ob_p = _sb_prompt(pq, kv_p, sb_bias, sb_norm3, l, bsz, seq)
        oa_s, nc_s, delta_s = _gdn_sample(pa, pq, sc_t, state_delta, conv_w, a_log, dt_bias, gdn_norm3, l, tp, dbsz,
                                          into=delta_s)
        q_s = pq[tp:, :GROUP_W].reshape(dbsz, 1, GROUP_W)
        ob_s = _sb_sample(q_s, cache_k4, cache_v4, page_table, bias_rows, sb_norm3, l)

        bufs = _outproj(oa_p, ob_p, w_out_b, x, ln1_g3, ln1_b3, l, alpha, 0)
        x1, x1b = _outproj(oa_s, ob_s, w_out_b, x, ln1_g3, ln1_b3, l, alpha, tp, into=bufs)

        if l % 2 == 0:
            i = l // 2
            f = _ffn(x1b, ones_gate, ffn_w_gate[i:i + 1], ffn_w_up[i:i + 1], ffn_w_down[i:i + 1])
        else:
            i = l // 2
            gates = _router(x1, router_pad, i, n_experts)
            f = _ffn(x1b, gates, moe_w_gate[i], moe_w_up[i], moe_w_down[i])
        final = functools.partial(_final, x1, f, p_all, ple_wg_b, ple_w_proj, ln2_g3, ln2_b3, l, alpha)
        if l + 1 < depth:
            x, xb = final()
        else:
            (y_prompt,) = final(row_off=0, n_rows=tp, with_bf16=False)
            (y_sample,) = final(row_off=tp, n_rows=dbsz, with_bf16=False)

        pa_p = pa[:tp, :3 * GROUP_W].reshape(bsz, seq, 3 * GROUP_W)
        conv_p.append(pa_p[:, seq - (CONV_K - 1):])
        delta_p.append(s_p)
        conv_s.append(jnp.swapaxes(nc_s, 0, 1))

    cache_p = lambda a: a.reshape(depth, bsz, seq, N_HEADS, HEAD_DIM)
    cache_s = lambda a: a.reshape(depth, dbsz, 1, N_HEADS, HEAD_DIM)
    st = jnp.stack
    return (y_prompt.reshape(bsz, seq, d), y_sample.reshape(dbsz, 1, d), st(conv_p), st(delta_p),
            cache_p(kv_cache_p[0]), cache_p(kv_cache_p[1]), st(conv_s), delta_s,
            cache_s(kv_cache_s[0]), cache_s(kv_cache_s[1]))
```

```python
import functools
import math

import jax
import jax.numpy as jnp
from jax import lax
from jax.experimental import pallas as pl
from jax.experimental.pallas import tpu as pltpu

F32 = jnp.float32
BF16 = jnp.bfloat16
HIGHEST = lax.Precision.HIGHEST

HEAD_DIM = 128
N_HEADS = 8
GROUP_W = N_HEADS * HEAD_DIM
CONV_K = 4
GDN_BASE_BLOCK = 16
LN_EPS = 1e-5
RMS_EPS = 1e-6
L2_EPS = 1e-6
SUBLANES = 8
LANES = 128
VMEM_BYTES_V7X = 64 * 1024 * 1024
VMEM_LIMIT = VMEM_BYTES_V7X * 7 // 8

_NT = (((1,), (1,)), ((), ()))
_BNN = (((2,), (1,)), ((0,), (0,)))
_BNT = (((2,), (2,)), ((0,), (0,)))
_BTN = (((1,), (1,)), ((0,), (0,)))


def _pick_tile(n, target, mult):
    best = None
    for t in range(mult, min(n, target) + 1, mult):
        if n % t == 0:
            best = t
    assert best is not None, (n, target, mult)
    return best


def _params(sem):
    return pltpu.CompilerParams(dimension_semantics=sem, vmem_limit_bytes=VMEM_LIMIT)


def _silu(x):
    return x * jax.nn.sigmoid(x)


def _softplus(x):
    return jnp.maximum(x, 0.0) + jnp.log1p(jnp.exp(-jnp.abs(x)))


def _dot(a, b):
    return jnp.dot(a, b, preferred_element_type=F32)


def _split_bf16(a):
    hi = a.astype(BF16)
    return hi, (a - hi.astype(F32)).astype(BF16)


def _dot1(a, b, dims):
    return lax.dot_general(a.astype(BF16), b.astype(BF16), dims, preferred_element_type=F32)


def _layer_norm(r, g, b):
    mu = jnp.mean(r, axis=-1, keepdims=True)
    c = r - mu
    var = jnp.mean(c * c, axis=-1, keepdims=True)
    return c * lax.rsqrt(var + LN_EPS) * g + b


def _rms_norm(x, g):
    return x * lax.rsqrt(jnp.mean(x * x, axis=-1, keepdims=True) + RMS_EPS) * g


def _l2norm(x):
    return x * lax.rsqrt(jnp.sum(x * x, axis=-1, keepdims=True) + L2_EPS)


def _matmul_kernel(x_ref, w_ref, o_ref):
    o_ref[...] = _dot(x_ref[...], w_ref[0])


def _matmul(xb, w, layer, tn):
    t, k = xb.shape
    n = w.shape[2]
    tm = _pick_tile(t, 1040, 16)
    return pl.pallas_call(
        _matmul_kernel,
        grid=(t // tm, n // tn),
        in_specs=[
            pl.BlockSpec((tm, k), lambda i, j: (i, 0)),
            pl.BlockSpec((1, k, tn), lambda i, j: (layer, 0, j)),
        ],
        out_specs=pl.BlockSpec((tm, tn), lambda i, j: (i, j)),
        out_shape=jax.ShapeDtypeStruct((t, n), F32),
        compiler_params=_params(("parallel", "arbitrary")),
        name="in_proj",
    )(xb, w)


def _kv_proj_kernel(x_ref, w_ref, *rest):
    kv_ref, k_ref, v_ref = rest[-3:]
    tm = x_ref.shape[0]
    res = _dot(x_ref[...], w_ref[0])
    kv_ref[...] = res
    for h in range(N_HEADS):
        lo = h * HEAD_DIM
        k_ref[0, pl.ds(h, tm, stride=N_HEADS), :] = res[:, lo:lo + HEAD_DIM]
        v_ref[0, pl.ds(h, tm, stride=N_HEADS), :] = res[:, GROUP_W + lo:GROUP_W + lo + HEAD_DIM]


def _kv_proj(xb, w_kv, layer, row_off, n_rows, into=None):
    depth, d, n = w_kv.shape
    tm = _pick_tile(n_rows, 512, 2 * SUBLANES)
    assert row_off % tm == 0 and n == 2 * GROUP_W
    off = row_off // tm
    in_specs = [
        pl.BlockSpec((tm, d), lambda i: (off + i, 0)),
        pl.BlockSpec((1, d, n), lambda i: (layer, 0, 0)),
    ]
    args = [xb, w_kv]
    aliases = {}
    if into is not None:
        in_specs += [pl.BlockSpec(memory_space=pl.ANY)] * 2
        aliases = {2: 1, 3: 2}
        args += list(into)
    cache_shape = jax.ShapeDtypeStruct((depth, n_rows * N_HEADS, HEAD_DIM), F32)
    cache_spec = pl.BlockSpec((1, tm * N_HEADS, HEAD_DIM), lambda i: (layer, i, 0))
    return pl.pallas_call(
        _kv_proj_kernel,
        grid=(n_rows // tm,),
        in_specs=in_specs,
        out_specs=[pl.BlockSpec((tm, n), lambda i: (i, 0)), cache_spec, cache_spec],
        out_shape=[jax.ShapeDtypeStruct((n_rows, n), F32), cache_shape, cache_shape],
        input_output_aliases=aliases,
        compiler_params=_params(("parallel",)),
        name="kv_proj",
    )(*args)


def _gdn_heads(qkv, gt, alog_ref, dtb_ref, layer):
    rows = qkv.shape[0]
    qs, ks, vs, betas, gls = [], [], [], [], []
    for h in range(N_HEADS):
        lo = h * HEAD_DIM
        qs.append(_l2norm(qkv[:, lo:lo + HEAD_DIM]) * (HEAD_DIM ** -0.5))
        ks.append(_l2norm(qkv[:, GROUP_W + lo:GROUP_W + lo + HEAD_DIM]))
        vs.append(qkv[:, 2 * GROUP_W + lo:2 * GROUP_W + lo + HEAD_DIM])
        b_logit = jnp.broadcast_to(gt[:, h:h + 1], (rows, HEAD_DIM))
        a_logit = jnp.broadcast_to(gt[:, N_HEADS + h:N_HEADS + h + 1], (rows, HEAD_DIM))
        betas.append(jax.nn.sigmoid(b_logit))
        rate = jnp.exp(jnp.full((1, HEAD_DIM), alog_ref[layer, h], F32))
        gls.append(-rate * _softplus(a_logit + dtb_ref[layer, h]))
    return tuple(jnp.stack(a, axis=0) for a in (qs, ks, vs, betas, gls))


def _gdn_prompt_kernel(alog_ref, dtb_ref, pa_ref, g_ref, cw_ref, norm_ref, o_ref, s_out_ref, s_ref, xp_ref, *,
                       layer, n_chunks):
    t = pl.program_id(1)
    c = pa_ref.shape[0]
    w3 = 3 * GROUP_W

    @pl.when(t == 0)
    def _():
        s_ref[...] = jnp.zeros_like(s_ref)
        xp_ref[0:SUBLANES, :] = jnp.zeros((SUBLANES, w3), F32)

    x = pa_ref[:, :w3]
    xp_ref[SUBLANES:SUBLANES + c, :] = x
    cw = cw_ref[0]
    acc = cw[CONV_K - 1:CONV_K, :] * x
    for back in range(1, CONV_K):
        acc = acc + cw[CONV_K - 1 - back:CONV_K - back, :] * xp_ref[SUBLANES - back:SUBLANES - back + c, :]
    xp_ref[0:SUBLANES, :] = x[c - SUBLANES:, :]
    qkv = _silu(acc)

    q, k, v, beta, gl = _gdn_heads(qkv, g_ref[...], alog_ref, dtb_ref, layer)

    row = lax.broadcasted_iota(jnp.int32, (c, c), 0)
    col = lax.broadcasted_iota(jnp.int32, (c, c), 1)
    incl = (col <= row)[None]
    strict = (col < row)[None]
    tri_incl = jnp.broadcast_to((col <= row).astype(BF16)[None], (N_HEADS, c, c))
    eye = (row == col).astype(F32)[None]

    gl_hi, gl_mid = _split_bf16(gl)
    gl_lo = (gl - gl_hi.astype(F32) - gl_mid.astype(F32)).astype(BF16)
    cdot = functools.partial(lax.dot_general, tri_incl, dimension_numbers=_BNN, preferred_element_type=F32)
    gc = cdot(gl_hi) + cdot(gl_mid) + cdot(gl_lo)
    gc_row = jnp.stack([gc[h].T for h in range(N_HEADS)], axis=0)
    decay = jnp.where(incl, jnp.exp(jnp.where(incl, gc - gc_row, 0.0)), 0.0)
    kb = k * beta
    vb = v * beta
    neg_a = -jnp.where(strict, _dot1(kb, k, _BNT) * decay, 0.0)
    same_block = lambda n: ((row // n) == (col // n))[None]
    power = jnp.where(same_block(GDN_BASE_BLOCK), neg_a, 0.0)
    inv = eye + power
    for _ in range(int(math.log2(GDN_BASE_BLOCK)) - 1):
        power = _dot1(power, power, _BNN)
        inv = inv + _dot1(inv, power, _BNN)
    blk = GDN_BASE_BLOCK
    while blk < c:
        below = jnp.where(jnp.logical_and(same_block(2 * blk), jnp.logical_not(same_block(blk))), neg_a, 0.0)
        inv = inv + _dot1(inv, _dot1(below, inv, _BNN), _BNN)
        blk *= 2
    eg = jnp.exp(gc)
    u = _dot1(inv, vb, _BNN)
    w = _dot1(inv, kb * eg, _BNN)
    a_intra = _dot1(q, k, _BNT) * decay
    g_last = gc[:, c - 1:c, :]
    k_dec = k * jnp.exp(g_last - gc)

    s = s_ref[...]
    v_new = u - _dot1(w, s, _BNN)
    o = _dot1(q * eg, s, _BNN) + _dot1(a_intra, v_new, _BNN)
    s_new = s * jnp.exp(g_last) + _dot1(k_dec, v_new, _BTN)
    s_ref[...] = s_new

    nrm = norm_ref[0]
    for h in range(N_HEADS):
        lo = h * HEAD_DIM
        z = pa_ref[:, w3 + lo:w3 + lo + HEAD_DIM]
        o_ref[:, lo:lo + HEAD_DIM] = _rms_norm(o[h], nrm) * _silu(z)

    @pl.when(t == n_chunks - 1)
    def _():
        s_out_ref[0] = s_new


def _gdn_prompt(pa, pq, conv_w, a_log, dt_bias, gdn_norm3, layer, bsz, seq):
    c = HEAD_DIM
    nt = seq // c
    gate_blk = GROUP_W // LANES
    return pl.pallas_call(
        functools.partial(_gdn_prompt_kernel, layer=layer, n_chunks=nt),
        grid=(bsz, nt),
        in_specs=[
            pl.BlockSpec(memory_space=pltpu.SMEM),
            pl.BlockSpec(memory_space=pltpu.SMEM),
            pl.BlockSpec((c, 4 * GROUP_W), lambda b, t: (b * nt + t, 0)),
            pl.BlockSpec((c, LANES), lambda b, t: (b * nt + t, gate_blk)),
            pl.BlockSpec((1, CONV_K, 3 * GROUP_W), lambda b, t: (layer, 0, 0)),
            pl.BlockSpec((1, 1, HEAD_DIM), lambda b, t: (layer, 0, 0)),
        ],
        out_specs=[
            pl.BlockSpec((c, GROUP_W), lambda b, t: (b * nt + t, 0)),
            pl.BlockSpec((1, N_HEADS, HEAD_DIM, HEAD_DIM), lambda b, t: (b, 0, 0, 0)),
        ],
        out_shape=[
            jax.ShapeDtypeStruct((bsz * seq, GROUP_W), F32),
            jax.ShapeDtypeStruct((bsz, N_HEADS, HEAD_DIM, HEAD_DIM), F32),
        ],
        scratch_shapes=[
            pltpu.VMEM((N_HEADS, HEAD_DIM, HEAD_DIM), F32),
            pltpu.VMEM((SUBLANES + c, 3 * GROUP_W), F32),
        ],
        compiler_params=_params(("parallel", "arbitrary")),
        name="gdn_prompt",
    )(a_log, dt_bias, pa, pq, conv_w, gdn_norm3)


def _gdn_sample_kernel(alog_ref, dtb_ref, pa_ref, g_ref, sc_ref, cw_ref, norm_ref, s_ref, *rest, layer):
    o_ref, nc_ref, s_out_ref = rest[-3:]
    bb = pa_ref.shape[0]
    w3 = 3 * GROUP_W
    x = pa_ref[:, :w3]
    cw = cw_ref[0]
    acc = cw[CONV_K - 1:CONV_K, :] * x
    for i in range(CONV_K - 1):
        acc = acc + cw[i:i + 1, :] * sc_ref[0, i]
    for i in range(CONV_K - 2):
        nc_ref[i] = sc_ref[0, i + 1]
    nc_ref[CONV_K - 2] = x
    qkv = _silu(acc)

    q, k, v, beta, gl = _gdn_heads(qkv, g_ref[...], alog_ref, dtb_ref, layer)
    eg = jnp.exp(gl)
    kq_t = jnp.concatenate([k.reshape(N_HEADS * bb, HEAD_DIM), q.reshape(N_HEADS * bb, HEAD_DIM)], axis=0).T
    nrm = norm_ref[0]
    for h in range(N_HEADS):
        lo = h * HEAD_DIM
        z = pa_ref[:, w3 + lo:w3 + lo + HEAD_DIM]
        for b in range(bb):
            p = h * bb + b
            kcol = kq_t[:, p:p + 1]
            qcol = kq_t[:, N_HEADS * bb + p:N_HEADS * bb + p + 1]
            s = s_ref[0, b, h]
            ks = jnp.sum(kcol * s, axis=0, keepdims=True)
            qs = jnp.sum(qcol * s, axis=0, keepdims=True)
            e = eg[h, b:b + 1]
            bt = beta[h, b:b + 1]
            v_new = bt * v[h, b:b + 1] - bt * e * ks
            qk = jnp.sum(q[h, b:b + 1] * k[h, b:b + 1], axis=1, keepdims=True)
            o = e * qs + qk * v_new
            s_out_ref[0, b, h] = s * e + kcol * v_new
            o_ref[b:b + 1, lo:lo + HEAD_DIM] = _rms_norm(o, nrm) * _silu(z[b:b + 1])


def _gdn_sample(pa, pq, sc_t, state_delta, conv_w, a_log, dt_bias, gdn_norm3, layer, t_prompt, dbsz, into=None):
    bb = LANES // (2 * N_HEADS)
    assert dbsz % bb == 0 and t_prompt % bb == 0
    off = t_prompt // bb
    gate_blk = GROUP_W // LANES
    state_spec = pl.BlockSpec((1, bb, N_HEADS, HEAD_DIM, HEAD_DIM), lambda i: (layer, i, 0, 0, 0))
    in_specs = [
        pl.BlockSpec(memory_space=pltpu.SMEM),
        pl.BlockSpec(memory_space=pltpu.SMEM),
        pl.BlockSpec((bb, 4 * GROUP_W), lambda i: (off + i, 0)),
        pl.BlockSpec((bb, LANES), lambda i: (off + i, gate_blk)),
        pl.BlockSpec((1, CONV_K - 1, bb, 3 * GROUP_W), lambda i: (layer, 0, i, 0)),
        pl.BlockSpec((1, CONV_K, 3 * GROUP_W), lambda i: (layer, 0, 0)),
        pl.BlockSpec((1, 1, HEAD_DIM), lambda i: (layer, 0, 0)),
        state_spec,
    ]
    args = [a_log, dt_bias, pa, pq, sc_t, conv_w, gdn_norm3, state_delta]
    aliases = {}
    if into is not None:
        in_specs.append(pl.BlockSpec(memory_space=pl.ANY))
        aliases = {len(args): 2}
        args.append(into)
    return pl.pallas_call(
        functools.partial(_gdn_sample_kernel, layer=layer),
        grid=(dbsz // bb,),
        in_specs=in_specs,
        out_specs=[
            pl.BlockSpec((bb, GROUP_W), lambda i: (i, 0)),
            pl.BlockSpec((CONV_K - 1, bb, 3 * GROUP_W), lambda i: (0, i, 0)),
            state_spec,
        ],
        out_shape=[
            jax.ShapeDtypeStruct((dbsz, GROUP_W), F32),
            jax.ShapeDtypeStruct((CONV_K - 1, dbsz, 3 * GROUP_W), F32),
            jax.ShapeDtypeStruct(state_delta.shape, F32),
        ],
        input_output_aliases=aliases,
        compiler_params=_params(("parallel",)),
        name="gdn_sample",
    )(*args)


def _suffix_sum_matrix(keys):
    row = lax.broadcasted_iota(jnp.int32, (keys, 2 * keys), 0)
    col = lax.broadcasted_iota(jnp.int32, (keys, 2 * keys), 1)
    return jnp.logical_or(row > col, col >= keys).astype(BF16)


def _stick_spend(z):
    return jnp.maximum(z, 0.0) + jnp.log(1.0 + jnp.exp(-jnp.abs(z)))


def _sb_weights(z, spend, spend_masked, r, sum_mat, n_tiles, axis):
    keys = sum_mat.shape[0]
    size = z.shape[axis] // n_tiles

    def tile(m, c):
        return lax.slice_in_dim(m, c * size, (c + 1) * size, axis=axis)

    rows = tile(z, 0).shape[0]
    hi, lo = _split_bf16(spend_masked)
    stack = lambda m: jnp.concatenate([tile(m, c) for c in range(n_tiles)], axis=0)
    sums = _dot(jnp.concatenate([stack(hi), stack(lo)], axis=1), jnp.concatenate([sum_mat, sum_mat], axis=0))
    parts = [None] * n_tiles
    for c in reversed(range(n_tiles)):
        s_c = sums[c * rows:(c + 1) * rows]
        parts[c] = jnp.exp(tile(z, c) - tile(spend, c) - s_c[:, :keys] - r)
        r = r + s_c[:, keys:]
    return parts, r


def _sb_prompt_kernel(bias_ref, q_ref, k_ref, v_ref, norm_ref, o_ref, *, layer, chunk):
    h = pl.program_id(1)
    i = pl.program_id(2)
    tq = q_ref.shape[0]
    n_sub = tq // chunk
    qb = (q_ref[...] * (HEAD_DIM ** -0.5)).astype(BF16)
    bias = bias_ref[layer, h]
    sum_mat = _suffix_sum_matrix(chunk)

    def block(start, r, acc, causal):
        kblk = k_ref[pl.ds(start, tq), :].astype(BF16)
        vblk = v_ref[pl.ds(start, tq), :].astype(BF16)
        z = lax.dot_general(qb, kblk, _NT, preferred_element_type=F32) + bias
        spend = _stick_spend(z)
        masked = spend if causal is None else jnp.where(causal, spend, 0.0)
        parts, r = _sb_weights(z, spend, masked, r, sum_mat, n_sub, 1)
        a = jnp.concatenate(parts, axis=1)
        if causal is not None:
            a = jnp.where(causal, a, 0.0)
        return r, acc + _dot(a.astype(BF16), vblk)

    row = lax.broadcasted_iota(jnp.int32, (tq, tq), 0)
    col = lax.broadcasted_iota(jnp.int32, (tq, tq), 1)
    init = (jnp.zeros((tq, chunk), F32), jnp.zeros((tq, HEAD_DIM), F32))
    carry = block(pl.multiple_of(i * tq, tq), *init, col < row)

    def older(j, carry):
        return block(pl.multiple_of(j * tq, tq), *carry, None)

    odd = i % 2
    carry = lax.cond(odd == 1, lambda c: older(i - 1, c), lambda c: c, carry)

    def body(p, carry):
        j = i - odd - 1 - 2 * p
        return older(j - 1, older(j, carry))

    _, acc = lax.fori_loop(0, (i - odd) // 2, body, carry)
    o_ref[...] = _rms_norm(acc, norm_ref[0])


def _sb_prompt(pq, kv, sb_bias, sb_norm3, layer, bsz, seq):
    tq = _pick_tile(seq, 512, LANES)
    nq = seq // tq
    return pl.pallas_call(
        functools.partial(_sb_prompt_kernel, layer=layer, chunk=LANES),
        grid=(bsz, N_HEADS, nq),
        in_specs=[
            pl.BlockSpec(memory_space=pltpu.SMEM),
            pl.BlockSpec((tq, HEAD_DIM), lambda b, h, i: (b * nq + i, h)),
            pl.BlockSpec((seq, HEAD_DIM), lambda b, h, i: (b, h)),
            pl.BlockSpec((seq, HEAD_DIM), lambda b, h, i: (b, N_HEADS + h)),
            pl.BlockSpec((1, 1, HEAD_DIM), lambda b, h, i: (layer, 0, 0)),
        ],
        out_specs=pl.BlockSpec((tq, HEAD_DIM), lambda b, h, i: (b * nq + i, h)),
        out_shape=jax.ShapeDtypeStruct((bsz * seq, GROUP_W), F32),
        compiler_params=_params(("parallel", "parallel", "arbitrary")),
        name="sb_prompt",
    )(sb_bias, pq, kv, kv, sb_norm3)


def _sb_sample_kernel(pt_ref, q_ref, bias_ref, norm_ref, *refs, n_pages):
    del pt_ref
    k_refs = refs[:n_pages]
    v_refs = refs[n_pages:2 * n_pages]
    o_ref = refs[2 * n_pages]
    page = k_refs[0].shape[2] // N_HEADS
    rows = 2 * SUBLANES
    q = jnp.broadcast_to(q_ref[0] * (HEAD_DIM ** -0.5), (rows, GROUP_W))
    head_of_row = lax.broadcasted_iota(jnp.int32, (rows, GROUP_W), 0)
    head_of_lane = lax.broadcasted_iota(jnp.int32, (rows, GROUP_W), 1) // HEAD_DIM
    q_heads = jnp.where(head_of_row == head_of_lane, q, 0.0).astype(BF16)
    bias = bias_ref[0]
    sum_mat = _suffix_sum_matrix(page)

    def head_rows(refs, h):
        return jnp.concatenate([ref[0, 0, pl.ds(h, page, stride=N_HEADS), :].astype(BF16) for ref in refs], axis=0)

    z = lax.dot_general(q_heads[:, :HEAD_DIM], head_rows(k_refs, 0), _NT, preferred_element_type=F32)
    for h in range(1, N_HEADS):
        z = z + lax.dot_general(q_heads[:, h * HEAD_DIM:(h + 1) * HEAD_DIM], head_rows(k_refs, h), _NT,
                                preferred_element_type=F32)
    z = z + jnp.concatenate([bias] * n_pages, axis=1)
    spend = _stick_spend(z)
    parts, _ = _sb_weights(z, spend, spend, jnp.zeros((rows, page), F32), sum_mat, n_pages, 1)
    a = jnp.concatenate(parts, axis=1).astype(BF16)
    nrm = norm_ref[0]
    for h in range(N_HEADS):
        o = _dot(a, head_rows(v_refs, h))
        o_ref[0, :, h * HEAD_DIM:(h + 1) * HEAD_DIM] = _rms_norm(o[h:h + 1], nrm)


def _sb_sample(q3, cache_k4, cache_v4, page_table, bias_rows, sb_norm3, layer):
    dbsz, n_pages = page_table.shape
    page_rows = cache_k4.shape[2]
    assert page_rows == LANES * N_HEADS

    def page_spec(p):
        return pl.BlockSpec((1, 1, page_rows, HEAD_DIM), lambda b, pt: (layer, pt[b, p], 0, 0))

    grid_spec = pltpu.PrefetchScalarGridSpec(
        num_scalar_prefetch=1,
        grid=(dbsz,),
        in_specs=[
            pl.BlockSpec((1, 1, GROUP_W), lambda b, pt: (b, 0, 0)),
            pl.BlockSpec((1, 2 * SUBLANES, LANES), lambda b, pt: (layer, 0, 0)),
            pl.BlockSpec((1, 1, HEAD_DIM), lambda b, pt: (layer, 0, 0)),
        ] + [page_spec(p) for p in range(n_pages)] * 2,
        out_specs=pl.BlockSpec((1, 1, GROUP_W), lambda b, pt: (b, 0, 0)),
    )
    out = pl.pallas_call(
        functools.partial(_sb_sample_kernel, n_pages=n_pages),
        grid_spec=grid_spec,
        out_shape=jax.ShapeDtypeStruct((dbsz, 1, GROUP_W), F32),
        compiler_params=_params(("parallel",)),
        name="sb_sample",
    )(page_table, q3, bias_rows, sb_norm3, *([cache_k4] * n_pages), *([cache_v4] * n_pages))
    return out.reshape(dbsz, GROUP_W)


def _outproj_kernel(oa_ref, ob_ref, w_ref, x_ref, g_ref, b_ref, *rest, alpha):
    o_ref, ob16_ref = rest[-2:]
    y = _dot(oa_ref[...].astype(BF16), w_ref[0, :GROUP_W, :]) + _dot(ob_ref[...].astype(BF16), w_ref[0, GROUP_W:, :])
    xn = _layer_norm(alpha * x_ref[...] + y, g_ref[0], b_ref[0])
    o_ref[...] = xn
    ob16_ref[...] = xn.astype(BF16)


def _outproj(oa, ob, w_out_b, x, x_off, ln_g3, ln_b3, layer, alpha, t_all, row_off, into=None):
    d = x.shape[1]
    n = oa.shape[0]
    tm = _pick_tile(n, 512, 16)
    assert row_off % tm == 0 and x_off % tm == 0
    off = row_off // tm
    res_off = x_off // tm
    in_specs = [
        pl.BlockSpec((tm, GROUP_W), lambda i: (i, 0)),
        pl.BlockSpec((tm, GROUP_W), lambda i: (i, 0)),
        pl.BlockSpec((1, 2 * GROUP_W, d), lambda i: (layer, 0, 0)),
        pl.BlockSpec((tm, d), lambda i: (res_off + i, 0)),
        pl.BlockSpec((1, 1, d), lambda i: (layer, 0, 0)),
        pl.BlockSpec((1, 1, d), lambda i: (layer, 0, 0)),
    ]
    args = [oa, ob, w_out_b, x, ln_g3, ln_b3]
    aliases = {}
    if into is not None:
        in_specs += [pl.BlockSpec(memory_space=pl.ANY)] * 2
        aliases = {len(args): 0, len(args) + 1: 1}
        args += list(into)
    return pl.pallas_call(
        functools.partial(_outproj_kernel, alpha=alpha),
        grid=(n // tm,),
        in_specs=in_specs,
        out_specs=[pl.BlockSpec((tm, d), lambda i: (off + i, 0))] * 2,
        out_shape=[jax.ShapeDtypeStruct((t_all, d), F32), jax.ShapeDtypeStruct((t_all, d), BF16)],
        input_output_aliases=aliases,
        compiler_params=_params(("parallel",)),
        name="out_proj_ln",
    )(*args)


def _router_kernel(x_ref, w_ref, o_ref, *, n_experts):
    logits = lax.dot_general(x_ref[...], w_ref[0], (((1,), (0,)), ((), ())), precision=HIGHEST,
                             preferred_element_type=F32)
    lane = lax.broadcasted_iota(jnp.int32, logits.shape, 1).astype(F32)
    neg = jnp.float32(-jnp.inf)
    logits = jnp.where(lane < n_experts, logits, neg)
    m1 = jnp.max(logits, axis=1, keepdims=True)
    i1 = jnp.min(jnp.where(logits == m1, lane, float(LANES)), axis=1, keepdims=True)
    rest = jnp.where(lane == i1, neg, logits)
    m2 = jnp.max(rest, axis=1, keepdims=True)
    i2 = jnp.min(jnp.where(rest == m2, lane, float(LANES)), axis=1, keepdims=True)
    e2 = jnp.exp(m2 - m1)
    denom = 1.0 + e2
    o_ref[...] = jnp.where(lane == i1, 1.0 / denom, 0.0) + jnp.where(lane == i2, e2 / denom, 0.0)


def _router(x, router_pad, layer_idx, n_experts):
    t, d = x.shape
    tm = _pick_tile(t, 1040, 8)
    return pl.pallas_call(
        functools.partial(_router_kernel, n_experts=n_experts),
        grid=(t // tm,),
        in_specs=[
            pl.BlockSpec((tm, d), lambda i: (i, 0)),
            pl.BlockSpec((1, d, LANES), lambda i: (layer_idx, 0, 0)),
        ],
        out_specs=pl.BlockSpec((tm, LANES), lambda i: (i, 0)),
        out_shape=jax.ShapeDtypeStruct((t, LANES), F32),
        compiler_params=_params(("parallel",)),
        name="moe_router",
    )(x, router_pad)


def _ffn_kernel(xb_ref, gate_ref, wg_ref, wu_ref, wd_ref, o_ref):
    e = pl.program_id(1)
    j = pl.program_id(2)

    @pl.when(jnp.logical_and(e == 0, j == 0))
    def _():
        o_ref[...] = jnp.zeros_like(o_ref)

    xb = xb_ref[...]
    g = _dot(xb, wg_ref[0].astype(BF16))
    u = _dot(xb, wu_ref[0].astype(BF16))
    gates = gate_ref[...]
    lane = lax.broadcasted_iota(jnp.int32, gates.shape, 1)
    gate_e = jnp.sum(jnp.where(lane == e, gates, 0.0), axis=1, keepdims=True)
    hidden = (_silu(g) * u * gate_e).astype(BF16)
    o_ref[...] += _dot(hidden, wd_ref[0].astype(BF16))


def _ffn(xb, gates, wg, wu, wd):
    t, d = xb.shape
    n_e, _, f = wg.shape
    tm = _pick_tile(t, 832, 16)
    tf = _pick_tile(f, 512, LANES)
    return pl.pallas_call(
        _ffn_kernel,
        grid=(t // tm, n_e, f // tf),
        in_specs=[
            pl.BlockSpec((tm, d), lambda i, e, j: (i, 0)),
            pl.BlockSpec((tm, LANES), lambda i, e, j: (i, 0)),
            pl.BlockSpec((1, d, tf), lambda i, e, j: (e, 0, j)),
            pl.BlockSpec((1, d, tf), lambda i, e, j: (e, 0, j)),
            pl.BlockSpec((1, tf, d), lambda i, e, j: (e, j, 0)),
        ],
        out_specs=pl.BlockSpec((tm, d), lambda i, e, j: (i, 0)),
        out_shape=jax.ShapeDtypeStruct((t, d), F32),
        compiler_params=_params(("parallel", "arbitrary", "arbitrary")),
        name="ffn",
    )(xb, gates, wg, wu, wd)


def _final_kernel(x_ref, f_ref, p_ref, wg_ref, wp_ref, g_ref, b_ref, *rest, alpha, with_bf16):
    x = x_ref[...]
    gate = jax.nn.sigmoid(_dot(x.astype(BF16), wg_ref[0]))
    proj = _dot(p_ref[0].astype(BF16), wp_ref[0].astype(BF16))
    xn = _layer_norm(alpha * x + f_ref[...] + gate * proj, g_ref[0], b_ref[0])
    if with_bf16:
        rest[-2][...] = xn
        rest[-1][...] = xn.astype(BF16)
    else:
        rest[-1][...] = xn


def _final(x, f, p_group, ple_wg_b, ple_wp, ln_g3, ln_b3, layer, alpha, row_off, merged, into=None):
    t, d = x.shape
    _, n_rows, pd = p_group.shape
    tm = _pick_tile(math.gcd(n_rows, row_off) if row_off else n_rows, 320, 16)
    off = row_off // tm
    row_spec = pl.BlockSpec((tm, d), lambda i: (off + i, 0))
    in_specs = [
        row_spec,
        row_spec,
        pl.BlockSpec((1, tm, pd), lambda i: (layer, i, 0)),
        pl.BlockSpec((1, d, d), lambda i: (layer, 0, 0)),
        pl.BlockSpec((1, pd, d), lambda i: (layer, 0, 0)),
        pl.BlockSpec((1, 1, d), lambda i: (layer, 0, 0)),
        pl.BlockSpec((1, 1, d), lambda i: (layer, 0, 0)),
    ]
    args = [x, f, p_group, ple_wg_b, ple_wp, ln_g3, ln_b3]
    aliases = {}
    if merged:
        out_specs = [row_spec] * 2
        out_shape = [jax.ShapeDtypeStruct((t, d), F32), jax.ShapeDtypeStruct((t, d), BF16)]
        if into is not None:
            in_specs += [pl.BlockSpec(memory_space=pl.ANY)] * 2
            aliases = {len(args): 0, len(args) + 1: 1}
            args += list(into)
    else:
        out_specs = [pl.BlockSpec((tm, d), lambda i: (i, 0))]
        out_shape = [jax.ShapeDtypeStruct((n_rows, d), F32)]
    return pl.pallas_call(
        functools.partial(_final_kernel, alpha=alpha, with_bf16=merged),
        grid=(n_rows // tm,),
        in_specs=in_specs,
        out_specs=out_specs,
        out_shape=out_shape,
        input_output_aliases=aliases,
        compiler_params=_params(("parallel",)),
        name="ple_ln",
    )(*args)


def kernel(x_prompt, x_sample, cache_k, cache_v, state_conv, state_delta, page_table, p_prompt, p_sample, w_in, conv_w, a_log, dt_bias, gdn_norm, sb_norm, sb_bias, w_out, ln1_g, ln1_b, ln2_g, ln2_b, ffn_w_gate, ffn_w_up, ffn_w_down, moe_router, moe_w_gate, moe_w_up, moe_w_down, ple_w_gate, ple_w_proj):
    depth, d, in_dim = w_in.shape
    bsz, seq, _ = x_prompt.shape
    dbsz = x_sample.shape[0]
    assert x_sample.shape[1] == 1 and seq % HEAD_DIM == 0
    assert in_dim == 7 * GROUP_W + 2 * N_HEADS and w_out.shape[1] == 2 * GROUP_W
    tp = bsz * seq
    n_experts = moe_router.shape[2]
    alpha = (2 * depth) ** 0.25

    res_p, res_p_off = x_prompt.reshape(tp, d), 0
    res_s, res_s_off = x_sample.reshape(dbsz, d), 0
    xb = jnp.concatenate([res_p.astype(BF16), res_s.astype(BF16)], axis=0)
    pp = p_prompt.reshape(depth, tp, -1)
    ps = p_sample.reshape(depth, dbsz, -1)

    gate_lo = 4 * GROUP_W
    sb_lo = gate_lo + 2 * N_HEADS
    nq = GROUP_W + 256
    w_a = w_in[:, :, :gate_lo].astype(BF16)
    w_q = jnp.concatenate(
        [w_in[:, :, sb_lo:sb_lo + GROUP_W], w_in[:, :, gate_lo:sb_lo],
         jnp.zeros((depth, d, nq - GROUP_W - 2 * N_HEADS), F32)], axis=-1).astype(BF16)
    w_kv = w_in[:, :, sb_lo + GROUP_W:].astype(BF16)
    w_out_b = w_out.astype(BF16)
    ple_wg_b = ple_w_gate.astype(BF16)
    router_pad = jnp.pad(moe_router, ((0, 0), (0, 0), (0, LANES - n_experts)))
    bias_rows = jnp.broadcast_to(
        jnp.pad(sb_bias, ((0, 0), (0, 2 * SUBLANES - N_HEADS)))[:, :, None], (depth, 2 * SUBLANES, LANES))
    sc_t = jnp.swapaxes(state_conv, 1, 2)
    cache_k4 = cache_k.reshape(cache_k.shape[:2] + (-1, HEAD_DIM))
    cache_v4 = cache_v.reshape(cache_v.shape[:2] + (-1, HEAD_DIM))
    r3 = lambda a: a.reshape(depth, 1, a.shape[-1])
    gdn_norm3, sb_norm3 = r3(gdn_norm), r3(sb_norm)
    ln1_g3, ln1_b3, ln2_g3, ln2_b3 = r3(ln1_g), r3(ln1_b), r3(ln2_g), r3(ln2_b)
    ones_gate = jnp.ones((tp + dbsz, LANES), F32)

    conv_p, delta_p, conv_s = [], [], []
    kv_cache_p = kv_cache_s = delta_s = None
    for l in range(depth):
        pa = _matmul(xb, w_a, l, tn=2048)
        pq = _matmul(xb, w_q, l, tn=nq)
        kv_p, *kv_cache_p = _kv_proj(xb, w_kv, l, 0, tp, into=kv_cache_p)
        _, *kv_cache_s = _kv_proj(xb, w_kv, l, tp, dbsz, into=kv_cache_s)

        oa_p, s_p = _gdn_prompt(pa, pq, conv_w, a_log, dt_bias, gdn_norm3, l, bsz, seq)
        ob_p = _sb_prompt(pq, kv_p, sb_bias, sb_norm3, l, bsz, seq)
        oa_s, nc_s, delta_s = _gdn_sample(pa, pq, sc_t, state_delta, conv_w, a_log, dt_bias, gdn_norm3, l, tp, dbsz,
                                          into=delta_s)
        q_s = pq[tp:, :GROUP_W].reshape(dbsz, 1, GROUP_W)
        ob_s = _sb_sample(q_s, cache_k4, cache_v4, page_table, bias_rows, sb_norm3, l)

        bufs = _outproj(oa_p, ob_p, w_out_b, res_p, res_p_off, ln1_g3, ln1_b3, l, alpha, tp + dbsz, 0)
        x1, x1b = _outproj(oa_s, ob_s, w_out_b, res_s, res_s_off, ln1_g3, ln1_b3, l, alpha, tp + dbsz, tp, into=bufs)

        if l % 2 == 0:
            i = l // 2
            f = _ffn(x1b, ones_gate, ffn_w_gate[i:i + 1], ffn_w_up[i:i + 1], ffn_w_down[i:i + 1])
        else:
            i = l // 2
            gates = _router(x1, router_pad, i, n_experts)
            f = _ffn(x1b, gates, moe_w_gate[i], moe_w_up[i], moe_w_down[i])
        final = lambda p, row_off, **kw: _final(x1, f, p, ple_wg_b, ple_w_proj, ln2_g3, ln2_b3, l, alpha, row_off, **kw)
        if l + 1 < depth:
            bufs = final(pp, 0, merged=True)
            x, xb = final(ps, tp, merged=True, into=bufs)
            res_p, res_p_off, res_s, res_s_off = x, 0, x, tp
        else:
            (y_prompt,) = final(pp, 0, merged=False)
            (y_sample,) = final(ps, tp, merged=False)

        tail = CONV_K - 1
        conv_p.append(jnp.stack([pa[(b + 1) * seq - tail:(b + 1) * seq, :3 * GROUP_W] for b in range(bsz)]))
        delta_p.append(s_p)
        conv_s.append(jnp.swapaxes(nc_s, 0, 1))

    cache_p = lambda a: a.reshape(depth, bsz, seq, N_HEADS, HEAD_DIM)
    cache_s = lambda a: a.reshape(depth, dbsz, 1, N_HEADS, HEAD_DIM)
    st = jnp.stack
    return (y_prompt.reshape(bsz, seq, d), y_sample.reshape(dbsz, 1, d), st(conv_p), st(delta_p),
            cache_p(kv_cache_p[0]), cache_p(kv_cache_p[1]), st(conv_s), delta_s,
            cache_s(kv_cache_s[0]), cache_s(kv_cache_s[1]))
```

```python
import functools
import math

import jax
import jax.numpy as jnp
from jax import lax
from jax.experimental import pallas as pl
from jax.experimental.pallas import tpu as pltpu

F32 = jnp.float32
BF16 = jnp.bfloat16
HIGHEST = lax.Precision.HIGHEST

HEAD_DIM = 128
N_HEADS = 8
GROUP_W = N_HEADS * HEAD_DIM
CONV_K = 4
GDN_BASE_BLOCK = 16
LN_EPS = 1e-5
RMS_EPS = 1e-6
L2_EPS = 1e-6
SUBLANES = 8
LANES = 128
VMEM_BYTES_V7X = 64 * 1024 * 1024
VMEM_LIMIT = VMEM_BYTES_V7X * 7 // 8

_NT = (((1,), (1,)), ((), ()))
_BNN = (((2,), (1,)), ((0,), (0,)))
_BNT = (((2,), (2,)), ((0,), (0,)))
_BTN = (((1,), (1,)), ((0,), (0,)))


def _pick_tile(n, target, mult):
    best = None
    for t in range(mult, min(n, target) + 1, mult):
        if n % t == 0:
            best = t
    assert best is not None, (n, target, mult)
    return best


def _params(sem):
    return pltpu.CompilerParams(dimension_semantics=sem, vmem_limit_bytes=VMEM_LIMIT)


def _silu(x):
    return x * jax.nn.sigmoid(x)


def _softplus(x):
    return jnp.maximum(x, 0.0) + jnp.log1p(jnp.exp(-jnp.abs(x)))


def _dot(a, b):
    return jnp.dot(a, b, preferred_element_type=F32)


def _split_bf16(a):
    hi = a.astype(BF16)
    return hi, (a - hi.astype(F32)).astype(BF16)


def _dot1(a, b, dims):
    return lax.dot_general(a.astype(BF16), b.astype(BF16), dims, preferred_element_type=F32)


def _layer_norm(r, g, b):
    mu = jnp.mean(r, axis=-1, keepdims=True)
    c = r - mu
    var = jnp.mean(c * c, axis=-1, keepdims=True)
    return c * lax.rsqrt(var + LN_EPS) * g + b


def _rms_norm(x, g):
    return x * lax.rsqrt(jnp.mean(x * x, axis=-1, keepdims=True) + RMS_EPS) * g


def _l2norm(x):
    return x * lax.rsqrt(jnp.sum(x * x, axis=-1, keepdims=True) + L2_EPS)


def _matmul_kernel(x_ref, w_ref, o_ref):
    o_ref[...] = _dot(x_ref[...], w_ref[0])


def _matmul(xb, w, layer, n, tn):
    t, k = xb.shape
    assert n % tn == 0 and n <= w.shape[2]
    tm = _pick_tile(t, 1040, 16)
    return pl.pallas_call(
        _matmul_kernel,
        grid=(t // tm, n // tn),
        in_specs=[
            pl.BlockSpec((tm, k), lambda i, j: (i, 0)),
            pl.BlockSpec((1, k, tn), lambda i, j: (layer, 0, j)),
        ],
        out_specs=pl.BlockSpec((tm, tn), lambda i, j: (i, j)),
        out_shape=jax.ShapeDtypeStruct((t, n), F32),
        compiler_params=_params(("parallel", "arbitrary")),
        name="in_proj",
    )(xb, w)


def _kv_proj_kernel(x_ref, w_ref, *rest):
    kv_ref, k_ref, v_ref = rest[-3:]
    tm = x_ref.shape[0]
    res = _dot(x_ref[...], w_ref[0])
    kv_ref[...] = res
    for h in range(N_HEADS):
        lo = h * HEAD_DIM
        k_ref[0, pl.ds(h, tm, stride=N_HEADS), :] = res[:, lo:lo + HEAD_DIM]
        v_ref[0, pl.ds(h, tm, stride=N_HEADS), :] = res[:, GROUP_W + lo:GROUP_W + lo + HEAD_DIM]


def _kv_proj(xb, w_kv, layer, row_off, n_rows, into=None):
    depth, d, n = w_kv.shape
    tm = _pick_tile(n_rows, 512, 2 * SUBLANES)
    assert row_off % tm == 0 and n == 2 * GROUP_W
    off = row_off // tm
    in_specs = [
        pl.BlockSpec((tm, d), lambda i: (off + i, 0)),
        pl.BlockSpec((1, d, n), lambda i: (layer, 0, 0)),
    ]
    args = [xb, w_kv]
    aliases = {}
    if into is not None:
        in_specs += [pl.BlockSpec(memory_space=pl.ANY)] * 2
        aliases = {2: 1, 3: 2}
        args += list(into)
    cache_shape = jax.ShapeDtypeStruct((depth, n_rows * N_HEADS, HEAD_DIM), F32)
    cache_spec = pl.BlockSpec((1, tm * N_HEADS, HEAD_DIM), lambda i: (layer, i, 0))
    return pl.pallas_call(
        _kv_proj_kernel,
        grid=(n_rows // tm,),
        in_specs=in_specs,
        out_specs=[pl.BlockSpec((tm, n), lambda i: (i, 0)), cache_spec, cache_spec],
        out_shape=[jax.ShapeDtypeStruct((n_rows, n), F32), cache_shape, cache_shape],
        input_output_aliases=aliases,
        compiler_params=_params(("parallel",)),
        name="kv_proj",
    )(*args)


def _gdn_heads(qkv, gt, alog_ref, dtb_ref, layer):
    rows = qkv.shape[0]
    qs, ks, vs, betas, gls = [], [], [], [], []
    for h in range(N_HEADS):
        lo = h * HEAD_DIM
        qs.append(_l2norm(qkv[:, lo:lo + HEAD_DIM]) * (HEAD_DIM ** -0.5))
        ks.append(_l2norm(qkv[:, GROUP_W + lo:GROUP_W + lo + HEAD_DIM]))
        vs.append(qkv[:, 2 * GROUP_W + lo:2 * GROUP_W + lo + HEAD_DIM])
        b_logit = jnp.broadcast_to(gt[:, h:h + 1], (rows, HEAD_DIM))
        a_logit = jnp.broadcast_to(gt[:, N_HEADS + h:N_HEADS + h + 1], (rows, HEAD_DIM))
        betas.append(jax.nn.sigmoid(b_logit))
        rate = jnp.exp(jnp.full((1, HEAD_DIM), alog_ref[layer, h], F32))
        gls.append(-rate * _softplus(a_logit + dtb_ref[layer, h]))
    return tuple(jnp.stack(a, axis=0) for a in (qs, ks, vs, betas, gls))


def _gdn_prompt_kernel(alog_ref, dtb_ref, pa_ref, g_ref, cw_ref, norm_ref, o_ref, s_out_ref, s_ref, xp_ref, *,
                       layer, n_chunks):
    t = pl.program_id(1)
    c = pa_ref.shape[0]
    w3 = 3 * GROUP_W

    @pl.when(t == 0)
    def _():
        s_ref[...] = jnp.zeros_like(s_ref)
        xp_ref[0:SUBLANES, :] = jnp.zeros((SUBLANES, w3), F32)

    x = pa_ref[:, :w3]
    xp_ref[SUBLANES:SUBLANES + c, :] = x
    cw = cw_ref[0]
    acc = cw[CONV_K - 1:CONV_K, :] * x
    for back in range(1, CONV_K):
        acc = acc + cw[CONV_K - 1 - back:CONV_K - back, :] * xp_ref[SUBLANES - back:SUBLANES - back + c, :]
    xp_ref[0:SUBLANES, :] = x[c - SUBLANES:, :]
    qkv = _silu(acc)

    q, k, v, beta, gl = _gdn_heads(qkv, g_ref[...], alog_ref, dtb_ref, layer)

    row = lax.broadcasted_iota(jnp.int32, (c, c), 0)
    col = lax.broadcasted_iota(jnp.int32, (c, c), 1)
    incl = (col <= row)[None]
    strict = (col < row)[None]
    tri_incl = jnp.broadcast_to((col <= row).astype(BF16)[None], (N_HEADS, c, c))
    eye = (row == col).astype(F32)[None]

    gl_hi, gl_mid = _split_bf16(gl)
    gl_lo = (gl - gl_hi.astype(F32) - gl_mid.astype(F32)).astype(BF16)
    cdot = functools.partial(lax.dot_general, tri_incl, dimension_numbers=_BNN, preferred_element_type=F32)
    gc = cdot(gl_hi) + cdot(gl_mid) + cdot(gl_lo)
    gc_row = jnp.stack([gc[h].T for h in range(N_HEADS)], axis=0)
    decay = jnp.where(incl, jnp.exp(jnp.where(incl, gc - gc_row, 0.0)), 0.0)
    kb = k * beta
    vb = v * beta
    neg_a = -jnp.where(strict, _dot1(kb, k, _BNT) * decay, 0.0)
    same_block = lambda n: ((row // n) == (col // n))[None]
    power = jnp.where(same_block(GDN_BASE_BLOCK), neg_a, 0.0)
    inv = eye + power
    for _ in range(int(math.log2(GDN_BASE_BLOCK)) - 1):
        power = _dot1(power, power, _BNN)
        inv = inv + _dot1(inv, power, _BNN)
    blk = GDN_BASE_BLOCK
    while blk < c:
        below = jnp.where(jnp.logical_and(same_block(2 * blk), jnp.logical_not(same_block(blk))), neg_a, 0.0)
        inv = inv + _dot1(inv, _dot1(below, inv, _BNN), _BNN)
        blk *= 2
    eg = jnp.exp(gc)
    u = _dot1(inv, vb, _BNN)
    w = _dot1(inv, kb * eg, _BNN)
    a_intra = _dot1(q, k, _BNT) * decay
    g_last = gc[:, c - 1:c, :]
    k_dec = k * jnp.exp(g_last - gc)

    s = s_ref[...]
    v_new = u - _dot1(w, s, _BNN)
    o = _dot1(q * eg, s, _BNN) + _dot1(a_intra, v_new, _BNN)
    s_new = s * jnp.exp(g_last) + _dot1(k_dec, v_new, _BTN)
    s_ref[...] = s_new

    nrm = norm_ref[0]
    for h in range(N_HEADS):
        lo = h * HEAD_DIM
        z = pa_ref[:, w3 + lo:w3 + lo + HEAD_DIM]
        o_ref[:, lo:lo + HEAD_DIM] = _rms_norm(o[h], nrm) * _silu(z)

    @pl.when(t == n_chunks - 1)
    def _():
        s_out_ref[0] = s_new


def _gdn_prompt(pa, pq, conv_w, a_log, dt_bias, gdn_norm3, layer, bsz, seq):
    c = HEAD_DIM
    nt = seq // c
    gate_blk = GROUP_W // LANES
    return pl.pallas_call(
        functools.partial(_gdn_prompt_kernel, layer=layer, n_chunks=nt),
        grid=(bsz, nt),
        in_specs=[
            pl.BlockSpec(memory_space=pltpu.SMEM),
            pl.BlockSpec(memory_space=pltpu.SMEM),
            pl.BlockSpec((c, 4 * GROUP_W), lambda b, t: (b * nt + t, 0)),
            pl.BlockSpec((c, LANES), lambda b, t: (b * nt + t, gate_blk)),
            pl.BlockSpec((1, CONV_K, 3 * GROUP_W), lambda b, t: (layer, 0, 0)),
            pl.BlockSpec((1, 1, HEAD_DIM), lambda b, t: (layer, 0, 0)),
        ],
        out_specs=[
            pl.BlockSpec((c, GROUP_W), lambda b, t: (b * nt + t, 0)),
            pl.BlockSpec((1, N_HEADS, HEAD_DIM, HEAD_DIM), lambda b, t: (b, 0, 0, 0)),
        ],
        out_shape=[
            jax.ShapeDtypeStruct((bsz * seq, GROUP_W), F32),
            jax.ShapeDtypeStruct((bsz, N_HEADS, HEAD_DIM, HEAD_DIM), F32),
        ],
        scratch_shapes=[
            pltpu.VMEM((N_HEADS, HEAD_DIM, HEAD_DIM), F32),
            pltpu.VMEM((SUBLANES + c, 3 * GROUP_W), F32),
        ],
        compiler_params=_params(("parallel", "arbitrary")),
        name="gdn_prompt",
    )(a_log, dt_bias, pa, pq, conv_w, gdn_norm3)


def _gdn_sample_kernel(alog_ref, dtb_ref, pa_ref, g_ref, sc_ref, cw_ref, norm_ref, s_ref, *rest, layer):
    o_ref, nc_ref, s_out_ref = rest[-3:]
    bb = pa_ref.shape[0]
    w3 = 3 * GROUP_W
    x = pa_ref[:, :w3]
    cw = cw_ref[0]
    acc = cw[CONV_K - 1:CONV_K, :] * x
    for i in range(CONV_K - 1):
        acc = acc + cw[i:i + 1, :] * sc_ref[0, i]
    for i in range(CONV_K - 2):
        nc_ref[i] = sc_ref[0, i + 1]
    nc_ref[CONV_K - 2] = x
    qkv = _silu(acc)

    q, k, v, beta, gl = _gdn_heads(qkv, g_ref[...], alog_ref, dtb_ref, layer)
    eg = jnp.exp(gl)
    kq_t = jnp.concatenate([k.reshape(N_HEADS * bb, HEAD_DIM), q.reshape(N_HEADS * bb, HEAD_DIM)], axis=0).T
    nrm = norm_ref[0]
    for h in range(N_HEADS):
        lo = h * HEAD_DIM
        z = pa_ref[:, w3 + lo:w3 + lo + HEAD_DIM]
        kq_rows = jnp.concatenate([k[h], q[h]], axis=0).astype(BF16)
        for b in range(bb):
            kcol = kq_t[:, h * bb + b:h * bb + b + 1]
            s = s_ref[0, b, h]
            through_state = _dot(kq_rows, s.astype(BF16))
            ks = through_state[b:b + 1]
            qs = through_state[bb + b:bb + b + 1]
            e = eg[h, b:b + 1]
            bt = beta[h, b:b + 1]
            v_new = bt * v[h, b:b + 1] - bt * e * ks
            qk = jnp.sum(q[h, b:b + 1] * k[h, b:b + 1], axis=1, keepdims=True)
            o = e * qs + qk * v_new
            s_out_ref[0, b, h] = s * e + kcol * v_new
            o_ref[b:b + 1, lo:lo + HEAD_DIM] = _rms_norm(o, nrm) * _silu(z[b:b + 1])


def _gdn_sample(pa, pq, sc_t, state_delta, conv_w, a_log, dt_bias, gdn_norm3, layer, t_prompt, dbsz, into=None):
    bb = LANES // (2 * N_HEADS)
    assert dbsz % bb == 0 and t_prompt % bb == 0
    off = t_prompt // bb
    gate_blk = GROUP_W // LANES
    state_spec = pl.BlockSpec((1, bb, N_HEADS, HEAD_DIM, HEAD_DIM), lambda i: (layer, i, 0, 0, 0))
    in_specs = [
        pl.BlockSpec(memory_space=pltpu.SMEM),
        pl.BlockSpec(memory_space=pltpu.SMEM),
        pl.BlockSpec((bb, 4 * GROUP_W), lambda i: (off + i, 0)),
        pl.BlockSpec((bb, LANES), lambda i: (off + i, gate_blk)),
        pl.BlockSpec((1, CONV_K - 1, bb, 3 * GROUP_W), lambda i: (layer, 0, i, 0)),
        pl.BlockSpec((1, CONV_K, 3 * GROUP_W), lambda i: (layer, 0, 0)),
        pl.BlockSpec((1, 1, HEAD_DIM), lambda i: (layer, 0, 0)),
        state_spec,
    ]
    args = [a_log, dt_bias, pa, pq, sc_t, conv_w, gdn_norm3, state_delta]
    aliases = {}
    if into is not None:
        in_specs.append(pl.BlockSpec(memory_space=pl.ANY))
        aliases = {len(args): 2}
        args.append(into)
    return pl.pallas_call(
        functools.partial(_gdn_sample_kernel, layer=layer),
        grid=(dbsz // bb,),
        in_specs=in_specs,
        out_specs=[
            pl.BlockSpec((bb, GROUP_W), lambda i: (i, 0)),
            pl.BlockSpec((CONV_K - 1, bb, 3 * GROUP_W), lambda i: (0, i, 0)),
            state_spec,
        ],
        out_shape=[
            jax.ShapeDtypeStruct((dbsz, GROUP_W), F32),
            jax.ShapeDtypeStruct((CONV_K - 1, dbsz, 3 * GROUP_W), F32),
            jax.ShapeDtypeStruct(state_delta.shape, F32),
        ],
        input_output_aliases=aliases,
        compiler_params=_params(("parallel",)),
        name="gdn_sample",
    )(*args)


def _suffix_sum_matrix(keys):
    row = lax.broadcasted_iota(jnp.int32, (keys, 2 * keys), 0)
    col = lax.broadcasted_iota(jnp.int32, (keys, 2 * keys), 1)
    return jnp.logical_or(row > col, col >= keys).astype(BF16)


def _stick_spend(z):
    return jnp.maximum(z, 0.0) + jnp.log(1.0 + jnp.exp(-jnp.abs(z)))


def _sb_weights(z, spend, spend_masked, r, sum_mat, n_tiles):
    rows = z.shape[0]
    keys = sum_mat.shape[0]
    assert z.shape[1] == n_tiles * keys

    def tile(m, c):
        return m[:, c * keys:(c + 1) * keys]

    hi, lo = _split_bf16(spend_masked)
    stack = lambda m: jnp.concatenate([tile(m, c) for c in range(n_tiles)], axis=0)
    sums = _dot(jnp.concatenate([stack(hi), stack(lo)], axis=1), jnp.concatenate([sum_mat, sum_mat], axis=0))
    parts = [None] * n_tiles
    for c in reversed(range(n_tiles)):
        s_c = sums[c * rows:(c + 1) * rows]
        parts[c] = jnp.exp(tile(z, c) - tile(spend, c) - s_c[:, :keys] - r)
        r = r + s_c[:, keys:]
    return parts, r


def _sb_prompt_kernel(bias_ref, q_ref, k_ref, v_ref, norm_ref, o_ref, *, layer, chunk):
    h = pl.program_id(1)
    i = pl.program_id(2)
    tq = q_ref.shape[0]
    n_sub = tq // chunk
    qb = (q_ref[...] * (HEAD_DIM ** -0.5)).astype(BF16)
    bias = bias_ref[layer, h]
    sum_mat = _suffix_sum_matrix(chunk)

    def block(start, r, acc, causal):
        kblk = k_ref[pl.ds(start, tq), :].astype(BF16)
        vblk = v_ref[pl.ds(start, tq), :].astype(BF16)
        z = lax.dot_general(qb, kblk, _NT, preferred_element_type=F32) + bias
        spend = _stick_spend(z)
        masked = spend if causal is None else jnp.where(causal, spend, 0.0)
        parts, r = _sb_weights(z, spend, masked, r, sum_mat, n_sub)
        a = jnp.concatenate(parts, axis=1)
        if causal is not None:
            a = jnp.where(causal, a, 0.0)
        return r, acc + _dot(a.astype(BF16), vblk)

    row = lax.broadcasted_iota(jnp.int32, (tq, tq), 0)
    col = lax.broadcasted_iota(jnp.int32, (tq, tq), 1)
    init = (jnp.zeros((tq, chunk), F32), jnp.zeros((tq, HEAD_DIM), F32))
    carry = block(pl.multiple_of(i * tq, tq), *init, col < row)

    def older(j, carry):
        return block(pl.multiple_of(j * tq, tq), *carry, None)

    odd = i % 2
    carry = lax.cond(odd == 1, lambda c: older(i - 1, c), lambda c: c, carry)

    def body(p, carry):
        j = i - odd - 1 - 2 * p
        return older(j - 1, older(j, carry))

    _, acc = lax.fori_loop(0, (i - odd) // 2, body, carry)
    o_ref[...] = _rms_norm(acc, norm_ref[0])


def _sb_prompt(pq, kv, sb_bias, sb_norm3, layer, bsz, seq):
    tq = _pick_tile(seq, 512, LANES)
    nq = seq // tq
    return pl.pallas_call(
        functools.partial(_sb_prompt_kernel, layer=layer, chunk=LANES),
        grid=(bsz, N_HEADS, nq),
        in_specs=[
            pl.BlockSpec(memory_space=pltpu.SMEM),
            pl.BlockSpec((tq, HEAD_DIM), lambda b, h, i: (b * nq + i, h)),
            pl.BlockSpec((seq, HEAD_DIM), lambda b, h, i: (b, h)),
            pl.BlockSpec((seq, HEAD_DIM), lambda b, h, i: (b, N_HEADS + h)),
            pl.BlockSpec((1, 1, HEAD_DIM), lambda b, h, i: (layer, 0, 0)),
        ],
        out_specs=pl.BlockSpec((tq, HEAD_DIM), lambda b, h, i: (b * nq + i, h)),
        out_shape=jax.ShapeDtypeStruct((bsz * seq, GROUP_W), F32),
        compiler_params=_params(("parallel", "parallel", "arbitrary")),
        name="sb_prompt",
    )(sb_bias, pq, kv, kv, sb_norm3)


def _sb_sample_kernel(pt_ref, q_ref, bias_ref, norm_ref, *refs, n_pages):
    del pt_ref
    k_refs = refs[:n_pages]
    v_refs = refs[n_pages:2 * n_pages]
    o_ref = refs[2 * n_pages]
    page = k_refs[0].shape[2] // N_HEADS
    rows = 2 * SUBLANES
    q = jnp.broadcast_to(q_ref[0] * (HEAD_DIM ** -0.5), (rows, GROUP_W))
    head_of_row = lax.broadcasted_iota(jnp.int32, (rows, GROUP_W), 0)
    head_of_lane = lax.broadcasted_iota(jnp.int32, (rows, GROUP_W), 1) // HEAD_DIM
    q_heads = jnp.where(head_of_row == head_of_lane, q, 0.0).astype(BF16)
    bias = bias_ref[0]
    sum_mat = _suffix_sum_matrix(page)

    def head_rows(refs, h):
        return jnp.concatenate([ref[0, 0, pl.ds(h, page, stride=N_HEADS), :].astype(BF16) for ref in refs], axis=0)

    z = lax.dot_general(q_heads[:, :HEAD_DIM], head_rows(k_refs, 0), _NT, preferred_element_type=F32)
    for h in range(1, N_HEADS):
        z = z + lax.dot_general(q_heads[:, h * HEAD_DIM:(h + 1) * HEAD_DIM], head_rows(k_refs, h), _NT,
                                preferred_element_type=F32)
    z = z + jnp.concatenate([bias] * n_pages, axis=1)
    spend = _stick_spend(z)
    parts, _ = _sb_weights(z, spend, spend, jnp.zeros((rows, page), F32), sum_mat, n_pages)
    a = jnp.concatenate(parts, axis=1).astype(BF16)
    nrm = norm_ref[0]
    for h in range(N_HEADS):
        o = _dot(a, head_rows(v_refs, h))
        o_ref[0, :, h * HEAD_DIM:(h + 1) * HEAD_DIM] = _rms_norm(o[h:h + 1], nrm)


def _sb_sample(q3, cache_k4, cache_v4, page_table, bias_rows, sb_norm3, layer):
    dbsz, n_pages = page_table.shape
    page_rows = cache_k4.shape[2]
    assert page_rows == LANES * N_HEADS

    def page_spec(p):
        return pl.BlockSpec((1, 1, page_rows, HEAD_DIM), lambda b, pt: (layer, pt[b, p], 0, 0))

    grid_spec = pltpu.PrefetchScalarGridSpec(
        num_scalar_prefetch=1,
        grid=(dbsz,),
        in_specs=[
            pl.BlockSpec((1, 1, GROUP_W), lambda b, pt: (b, 0, 0)),
            pl.BlockSpec((1, 2 * SUBLANES, LANES), lambda b, pt: (layer, 0, 0)),
            pl.BlockSpec((1, 1, HEAD_DIM), lambda b, pt: (layer, 0, 0)),
        ] + [page_spec(p) for p in range(n_pages)] * 2,
        out_specs=pl.BlockSpec((1, 1, GROUP_W), lambda b, pt: (b, 0, 0)),
    )
    out = pl.pallas_call(
        functools.partial(_sb_sample_kernel, n_pages=n_pages),
        grid_spec=grid_spec,
        out_shape=jax.ShapeDtypeStruct((dbsz, 1, GROUP_W), F32),
        compiler_params=_params(("parallel",)),
        name="sb_sample",
    )(page_table, q3, bias_rows, sb_norm3, *([cache_k4] * n_pages), *([cache_v4] * n_pages))
    return out.reshape(dbsz, GROUP_W)


def _outproj_kernel(oa_ref, ob_ref, w_ref, x_ref, g_ref, b_ref, *rest, alpha):
    o_ref, ob16_ref = rest[-2:]
    y = _dot(oa_ref[...].astype(BF16), w_ref[0, :GROUP_W, :]) + _dot(ob_ref[...].astype(BF16), w_ref[0, GROUP_W:, :])
    xn = _layer_norm(alpha * x_ref[...] + y, g_ref[0], b_ref[0])
    o_ref[...] = xn
    ob16_ref[...] = xn.astype(BF16)


def _outproj(oa, ob, w_out_b, x, x_off, ln_g3, ln_b3, layer, alpha, t_all, row_off, into=None):
    d = x.shape[1]
    n = oa.shape[0]
    tm = _pick_tile(n, 512, 16)
    assert row_off % tm == 0 and x_off % tm == 0
    off = row_off // tm
    res_off = x_off // tm
    in_specs = [
        pl.BlockSpec((tm, GROUP_W), lambda i: (i, 0)),
        pl.BlockSpec((tm, GROUP_W), lambda i: (i, 0)),
        pl.BlockSpec((1, 2 * GROUP_W, d), lambda i: (layer, 0, 0)),
        pl.BlockSpec((tm, d), lambda i: (res_off + i, 0)),
        pl.BlockSpec((1, 1, d), lambda i: (layer, 0, 0)),
        pl.BlockSpec((1, 1, d), lambda i: (layer, 0, 0)),
    ]
    args = [oa, ob, w_out_b, x, ln_g3, ln_b3]
    aliases = {}
    if into is not None:
        in_specs += [pl.BlockSpec(memory_space=pl.ANY)] * 2
        aliases = {len(args): 0, len(args) + 1: 1}
        args += list(into)
    return pl.pallas_call(
        functools.partial(_outproj_kernel, alpha=alpha),
        grid=(n // tm,),
        in_specs=in_specs,
        out_specs=[pl.BlockSpec((tm, d), lambda i: (off + i, 0))] * 2,
        out_shape=[jax.ShapeDtypeStruct((t_all, d), F32), jax.ShapeDtypeStruct((t_all, d), BF16)],
        input_output_aliases=aliases,
        compiler_params=_params(("parallel",)),
        name="out_proj_ln",
    )(*args)


def _router_kernel(x_ref, w_ref, o_ref, *, n_experts):
    logits = lax.dot_general(x_ref[...], w_ref[0], (((1,), (0,)), ((), ())), precision=HIGHEST,
                             preferred_element_type=F32)
    lane = lax.broadcasted_iota(jnp.int32, logits.shape, 1).astype(F32)
    neg = jnp.float32(-jnp.inf)
    logits = jnp.where(lane < n_experts, logits, neg)
    m1 = jnp.max(logits, axis=1, keepdims=True)
    i1 = jnp.min(jnp.where(logits == m1, lane, float(LANES)), axis=1, keepdims=True)
    rest = jnp.where(lane == i1, neg, logits)
    m2 = jnp.max(rest, axis=1, keepdims=True)
    i2 = jnp.min(jnp.where(rest == m2, lane, float(LANES)), axis=1, keepdims=True)
    e2 = jnp.exp(m2 - m1)
    denom = 1.0 + e2
    o_ref[...] = jnp.where(lane == i1, 1.0 / denom, 0.0) + jnp.where(lane == i2, e2 / denom, 0.0)


def _router(x, router_pad, layer_idx, n_experts):
    t, d = x.shape
    tm = _pick_tile(t, 1040, 8)
    return pl.pallas_call(
        functools.partial(_router_kernel, n_experts=n_experts),
        grid=(t // tm,),
        in_specs=[
            pl.BlockSpec((tm, d), lambda i: (i, 0)),
            pl.BlockSpec((1, d, LANES), lambda i: (layer_idx, 0, 0)),
        ],
        out_specs=pl.BlockSpec((tm, LANES), lambda i: (i, 0)),
        out_shape=jax.ShapeDtypeStruct((t, LANES), F32),
        compiler_params=_params(("parallel",)),
        name="moe_router",
    )(x, router_pad)


def _ffn_kernel(xb_ref, gate_ref, wg_ref, wu_ref, wd_ref, o_ref):
    e = pl.program_id(1)
    j = pl.program_id(2)

    @pl.when(jnp.logical_and(e == 0, j == 0))
    def _():
        o_ref[...] = jnp.zeros_like(o_ref)

    xb = xb_ref[...]
    g = _dot(xb, wg_ref[0].astype(BF16))
    u = _dot(xb, wu_ref[0].astype(BF16))
    gates = gate_ref[...]
    lane = lax.broadcasted_iota(jnp.int32, gates.shape, 1)
    gate_e = jnp.sum(jnp.where(lane == e, gates, 0.0), axis=1, keepdims=True)
    hidden = (_silu(g) * u * gate_e).astype(BF16)
    o_ref[...] += _dot(hidden, wd_ref[0].astype(BF16))


def _ffn(xb, gates, wg, wu, wd):
    t, d = xb.shape
    n_e, _, f = wg.shape
    tm = _pick_tile(t, 832, 16)
    tf = _pick_tile(f, 512, LANES)
    return pl.pallas_call(
        _ffn_kernel,
        grid=(t // tm, n_e, f // tf),
        in_specs=[
            pl.BlockSpec((tm, d), lambda i, e, j: (i, 0)),
            pl.BlockSpec((tm, LANES), lambda i, e, j: (i, 0)),
            pl.BlockSpec((1, d, tf), lambda i, e, j: (e, 0, j)),
            pl.BlockSpec((1, d, tf), lambda i, e, j: (e, 0, j)),
            pl.BlockSpec((1, tf, d), lambda i, e, j: (e, j, 0)),
        ],
        out_specs=pl.BlockSpec((tm, d), lambda i, e, j: (i, 0)),
        out_shape=jax.ShapeDtypeStruct((t, d), F32),
        compiler_params=_params(("parallel", "arbitrary", "arbitrary")),
        name="ffn",
    )(xb, gates, wg, wu, wd)


def _final_kernel(x_ref, f_ref, p_ref, wg_ref, wp_ref, g_ref, b_ref, *rest, alpha, with_bf16):
    x = x_ref[...]
    gate = jax.nn.sigmoid(_dot(x.astype(BF16), wg_ref[0]))
    proj = _dot(p_ref[0].astype(BF16), wp_ref[0].astype(BF16))
    xn = _layer_norm(alpha * x + f_ref[...] + gate * proj, g_ref[0], b_ref[0])
    if with_bf16:
        rest[-2][...] = xn
        rest[-1][...] = xn.astype(BF16)
    else:
        rest[-1][...] = xn


def _final(x, f, p_group, ple_wg_b, ple_wp, ln_g3, ln_b3, layer, alpha, row_off, merged, into=None):
    t, d = x.shape
    _, n_rows, pd = p_group.shape
    tm = _pick_tile(math.gcd(n_rows, row_off) if row_off else n_rows, 320, 16)
    off = row_off // tm
    row_spec = pl.BlockSpec((tm, d), lambda i: (off + i, 0))
    in_specs = [
        row_spec,
        row_spec,
        pl.BlockSpec((1, tm, pd), lambda i: (layer, i, 0)),
        pl.BlockSpec((1, d, d), lambda i: (layer, 0, 0)),
        pl.BlockSpec((1, pd, d), lambda i: (layer, 0, 0)),
        pl.BlockSpec((1, 1, d), lambda i: (layer, 0, 0)),
        pl.BlockSpec((1, 1, d), lambda i: (layer, 0, 0)),
    ]
    args = [x, f, p_group, ple_wg_b, ple_wp, ln_g3, ln_b3]
    aliases = {}
    if merged:
        out_specs = [row_spec] * 2
        out_shape = [jax.ShapeDtypeStruct((t, d), F32), jax.ShapeDtypeStruct((t, d), BF16)]
        if into is not None:
            in_specs += [pl.BlockSpec(memory_space=pl.ANY)] * 2
            aliases = {len(args): 0, len(args) + 1: 1}
            args += list(into)
    else:
        out_specs = [pl.BlockSpec((tm, d), lambda i: (i, 0))]
        out_shape = [jax.ShapeDtypeStruct((n_rows, d), F32)]
    return pl.pallas_call(
        functools.partial(_final_kernel, alpha=alpha, with_bf16=merged),
        grid=(n_rows // tm,),
        in_specs=in_specs,
        out_specs=out_specs,
        out_shape=out_shape,
        input_output_aliases=aliases,
        compiler_params=_params(("parallel",)),
        name="ple_ln",
    )(*args)


def kernel(x_prompt, x_sample, cache_k, cache_v, state_conv, state_delta, page_table, p_prompt, p_sample, w_in, conv_w, a_log, dt_bias, gdn_norm, sb_norm, sb_bias, w_out, ln1_g, ln1_b, ln2_g, ln2_b, ffn_w_gate, ffn_w_up, ffn_w_down, moe_router, moe_w_gate, moe_w_up, moe_w_down, ple_w_gate, ple_w_proj):
    depth, d, in_dim = w_in.shape
    bsz, seq, _ = x_prompt.shape
    dbsz = x_sample.shape[0]
    assert x_sample.shape[1] == 1 and seq % HEAD_DIM == 0
    assert in_dim == 7 * GROUP_W + 2 * N_HEADS and w_out.shape[1] == 2 * GROUP_W
    tp = bsz * seq
    n_experts = moe_router.shape[2]
    alpha = (2 * depth) ** 0.25

    res_p, res_p_off = x_prompt.reshape(tp, d), 0
    res_s, res_s_off = x_sample.reshape(dbsz, d), 0
    xb = jnp.concatenate([res_p.astype(BF16), res_s.astype(BF16)], axis=0)
    pp = p_prompt.reshape(depth, tp, -1)
    ps = p_sample.reshape(depth, dbsz, -1)

    gate_lo = 4 * GROUP_W
    sb_lo = gate_lo + 2 * N_HEADS
    nq = GROUP_W + 256
    w_in_b = w_in.astype(BF16)
    w_q = jnp.concatenate(
        [w_in_b[:, :, sb_lo:sb_lo + GROUP_W], w_in_b[:, :, gate_lo:sb_lo],
         jnp.zeros((depth, d, nq - GROUP_W - 2 * N_HEADS), BF16)], axis=-1)
    w_kv = w_in_b[:, :, sb_lo + GROUP_W:]
    w_out_b = w_out.astype(BF16)
    ple_wg_b = ple_w_gate.astype(BF16)
    router_pad = jnp.pad(moe_router, ((0, 0), (0, 0), (0, LANES - n_experts)))
    bias_rows = jnp.broadcast_to(
        jnp.pad(sb_bias, ((0, 0), (0, 2 * SUBLANES - N_HEADS)))[:, :, None], (depth, 2 * SUBLANES, LANES))
    sc_t = jnp.swapaxes(state_conv, 1, 2)
    cache_k4 = cache_k.reshape(cache_k.shape[:2] + (-1, HEAD_DIM))
    cache_v4 = cache_v.reshape(cache_v.shape[:2] + (-1, HEAD_DIM))
    r3 = lambda a: a.reshape(depth, 1, a.shape[-1])
    gdn_norm3, sb_norm3 = r3(gdn_norm), r3(sb_norm)
    ln1_g3, ln1_b3, ln2_g3, ln2_b3 = r3(ln1_g), r3(ln1_b), r3(ln2_g), r3(ln2_b)
    ones_gate = jnp.ones((tp + dbsz, LANES), F32)

    conv_p, delta_p, conv_s = [], [], []
    kv_cache_p = kv_cache_s = delta_s = None
    for l in range(depth):
        pa = _matmul(xb, w_in_b, l, gate_lo, tn=2048)
        pq = _matmul(xb, w_q, l, nq, tn=nq)
        kv_p, *kv_cache_p = _kv_proj(xb, w_kv, l, 0, tp, into=kv_cache_p)
        _, *kv_cache_s = _kv_proj(xb, w_kv, l, tp, dbsz, into=kv_cache_s)

        oa_p, s_p = _gdn_prompt(pa, pq, conv_w, a_log, dt_bias, gdn_norm3, l, bsz, seq)
        ob_p = _sb_prompt(pq, kv_p, sb_bias, sb_norm3, l, bsz, seq)
        oa_s, nc_s, delta_s = _gdn_sample(pa, pq, sc_t, state_delta, conv_w, a_log, dt_bias, gdn_norm3, l, tp, dbsz,
                                          into=delta_s)
        q_s = pq[tp:, :GROUP_W].reshape(dbsz, 1, GROUP_W)
        ob_s = _sb_sample(q_s, cache_k4, cache_v4, page_table, bias_rows, sb_norm3, l)

        bufs = _outproj(oa_p, ob_p, w_out_b, res_p, res_p_off, ln1_g3, ln1_b3, l, alpha, tp + dbsz, 0)
        x1, x1b = _outproj(oa_s, ob_s, w_out_b, res_s, res_s_off, ln1_g3, ln1_b3, l, alpha, tp + dbsz, tp, into=bufs)

        if l % 2 == 0:
            i = l // 2
            f = _ffn(x1b, ones_gate, ffn_w_gate[i:i + 1], ffn_w_up[i:i + 1], ffn_w_down[i:i + 1])
        else:
            i = l // 2
            gates = _router(x1, router_pad, i, n_experts)
            f = _ffn(x1b, gates, moe_w_gate[i], moe_w_up[i], moe_w_down[i])
        final = lambda p, row_off, **kw: _final(x1, f, p, ple_wg_b, ple_w_proj, ln2_g3, ln2_b3, l, alpha, row_off, **kw)
        if l + 1 < depth:
            bufs = final(pp, 0, merged=True)
            x, xb = final(ps, tp, merged=True, into=bufs)
            res_p, res_p_off, res_s, res_s_off = x, 0, x, tp
        else:
            (y_prompt,) = final(pp, 0, merged=False)
            (y_sample,) = final(ps, tp, merged=False)

        tail = CONV_K - 1
        conv_p.append(jnp.stack([pa[(b + 1) * seq - tail:(b + 1) * seq, :3 * GROUP_W] for b in range(bsz)]))
        delta_p.append(s_p)
        conv_s.append(jnp.swapaxes(nc_s, 0, 1))

    cache_p = lambda a: a.reshape(depth, bsz, seq, N_HEADS, HEAD_DIM)
    cache_s = lambda a: a.reshape(depth, dbsz, 1, N_HEADS, HEAD_DIM)
    st = jnp.stack
    return (y_prompt.reshape(bsz, seq, d), y_sample.reshape(dbsz, 1, d), st(conv_p), st(delta_p),
            cache_p(kv_cache_p[0]), cache_p(kv_cache_p[1]), st(conv_s), delta_s,
            cache_s(kv_cache_s[0]), cache_s(kv_cache_s[1]))
```
